```python
import jax, jax.numpy as jnp
from jax import lax
import numpy as np

D_MODEL = 2048
BATCH = 4
SEQ = 2048
DEPTH = 2

N_A_LAYERS = DEPTH // 2
N_B_LAYERS = DEPTH - N_A_LAYERS
CONV_WIDTH = 3
N_HEADS = 16
HEAD_DIM = D_MODEL // N_HEADS
ROT_DIM = HEAD_DIM // 4
ROPE_THETA = 500000.0
MOBA_BLOCK = 256
MOBA_TOPK = 3
Q_CHUNK = 16
N_EXPERTS = 32
TOP_K = 4
D_FF = D_MODEL
SWIGLU_ALPHA = 1.702
SWIGLU_LIMIT = 7.0
NORM_EPS = 1e-5
NEG_INF = -1e30
N_ADA = 6

kernel_name = "yoco_shortconv_moba_moe_adaln"


def rmsnorm(x, g):
    xf = x.astype(jnp.float32)
    xf = xf * lax.rsqrt(jnp.mean(xf * xf, axis=-1, keepdims=True) + NORM_EPS)
    return xf.astype(x.dtype) * g


def modulate(x, g, shift, scale):
    return rmsnorm(x, g) * (1 + scale) + shift


def rope_tables(positions):
    inv = jnp.float32(ROPE_THETA) ** (-jnp.arange(0, ROT_DIM, 2, dtype=jnp.float32) / ROT_DIM)
    ang = positions.astype(jnp.float32)[..., None] * inv
    return jnp.cos(ang)[:, :, None, :], jnp.sin(ang)[:, :, None, :]


def apply_rope(x, cos, sin):
    half = ROT_DIM // 2
    x1 = x[..., :half].astype(jnp.float32)
    x2 = x[..., half:ROT_DIM].astype(jnp.float32)
    rot = jnp.concatenate([x1 * cos - x2 * sin, x2 * cos + x1 * sin], axis=-1).astype(x.dtype)
    return jnp.concatenate([rot, x[..., ROT_DIM:]], axis=-1)


def short_conv_mixer(h, w_in, w_conv, w_out):
    seq = h.shape[1]
    b_gate, c_gate, u = jnp.split(h @ w_in, 3, axis=-1)
    v = c_gate * u
    vp = jnp.pad(v, ((0, 0), (CONV_WIDTH - 1, 0), (0, 0)))
    conv = sum(w_conv[j] * vp[:, j:j + seq] for j in range(CONV_WIDTH))
    return (b_gate * conv) @ w_out


def shared_kv(x, kv_norm_g, w_kv, cos, sin):
    bsz, seq, _ = x.shape
    kv = (rmsnorm(x, kv_norm_g) @ w_kv).reshape(bsz, seq, 2, N_HEADS, HEAD_DIM)
    k = apply_rope(kv[:, :, 0], cos, sin).transpose(0, 2, 1, 3)
    v = kv[:, :, 1].transpose(0, 2, 1, 3)
    n_blocks = -(-seq // MOBA_BLOCK)
    pad = n_blocks * MOBA_BLOCK - seq
    k_pad = jnp.pad(k, ((0, 0), (0, 0), (0, pad), (0, 0)))
    v_pad = jnp.pad(v, ((0, 0), (0, 0), (0, pad), (0, 0)))
    k_mean = k_pad.astype(jnp.float32).reshape(bsz, N_HEADS, n_blocks, MOBA_BLOCK, HEAD_DIM).mean(axis=3)
    return k_pad, v_pad, k_mean


def moba_attention(h, w_q, w_o, k_pad, v_pad, k_mean, cos, sin):
    bsz, seq, _ = h.shape
    n_blocks = k_mean.shape[2]
    topk = min(MOBA_TOPK, n_blocks)
    q = apply_rope((h @ w_q).reshape(bsz, seq, N_HEADS, HEAD_DIM), cos, sin).transpose(0, 2, 1, 3)
    k_blk = k_pad.reshape(bsz, N_HEADS, n_blocks, MOBA_BLOCK, HEAD_DIM)
    v_blk = v_pad.reshape(bsz, N_HEADS, n_blocks, MOBA_BLOCK, HEAD_DIM)
    b_ix = jnp.arange(bsz)[:, None, None, None]
    h_ix = jnp.arange(N_HEADS)[None, :, None, None]
    scale = HEAD_DIM ** -0.5

    def attend_chunk(ci):
        q0 = ci * Q_CHUNK
        blk = q0 // MOBA_BLOCK
        q_c = lax.dynamic_slice_in_dim(q, q0, Q_CHUNK, axis=2)
        gate = jnp.einsum('bhqd,bhnd->bhqn', q_c.astype(jnp.float32), k_mean)
        gate = jnp.where(jnp.arange(n_blocks) < blk, gate, NEG_INF)
        _, sel = lax.top_k(gate, topk)
        sel_valid = sel < blk
        k_sel = k_blk[b_ix, h_ix, sel]
        v_sel = v_blk[b_ix, h_ix, sel]
        k_own = lax.dynamic_slice_in_dim(k_pad, blk * MOBA_BLOCK, MOBA_BLOCK, axis=2)
        v_own = lax.dynamic_slice_in_dim(v_pad, blk * MOBA_BLOCK, MOBA_BLOCK, axis=2)
        s_sel = jnp.einsum('bhqd,bhqjkd->bhqjk', q_c, k_sel, preferred_element_type=jnp.float32) * scale
        s_sel = jnp.where(sel_valid[..., None], s_sel, NEG_INF).reshape(bsz, N_HEADS, Q_CHUNK, topk * MOBA_BLOCK)
        s_own = jnp.einsum('bhqd,bhkd->bhqk', q_c, k_own, preferred_element_type=jnp.float32) * scale
        q_pos = q0 + jnp.arange(Q_CHUNK)
        k_pos = blk * MOBA_BLOCK + jnp.arange(MOBA_BLOCK)
        s_own = jnp.where(k_pos[None, :] <= q_pos[:, None], s_own, NEG_INF)
        p = jax.nn.softmax(jnp.concatenate([s_sel, s_own], axis=-1), axis=-1).astype(v_pad.dtype)
        p_sel = p[..., :topk * MOBA_BLOCK].reshape(bsz, N_HEADS, Q_CHUNK, topk, MOBA_BLOCK)
        p_own = p[..., topk * MOBA_BLOCK:]
        return (jnp.einsum('bhqjk,bhqjkd->bhqd', p_sel, v_sel)
                + jnp.einsum('bhqk,bhkd->bhqd', p_own, v_own))

    out = lax.map(attend_chunk, jnp.arange(seq // Q_CHUNK))
    out = out.transpose(1, 0, 3, 2, 4).reshape(bsz, seq, N_HEADS * HEAD_DIM)
    return out @ w_o


def moe_ffn(h, w_router, b_router, w_gu, b_gu, w_down, b_down):
    bsz, seq, d = h.shape
    t = h.reshape(bsz * seq, d)
    logits = (t @ w_router + b_router).astype(jnp.float32)
    top_val, top_idx = lax.top_k(logits, TOP_K)
    top_w = jax.nn.softmax(top_val, axis=-1)
    comb = jnp.sum(jax.nn.one_hot(top_idx, N_EXPERTS, dtype=jnp.float32) * top_w[..., None], axis=-2)
    comb = comb.astype(h.dtype)

    def expert_step(acc, params):
        wgu, bgu, wd, bd, g = params
        gu = t @ wgu + bgu
        gate = jnp.minimum(gu[:, :D_FF], SWIGLU_LIMIT)
        up = jnp.clip(gu[:, D_FF:], -SWIGLU_LIMIT, SWIGLU_LIMIT)
        glu = gate * jax.nn.sigmoid(gate * SWIGLU_ALPHA)
        out = ((up + 1) * glu) @ wd + bd
        return acc + g[:, None] * out, None

    y, _ = lax.scan(expert_step, jnp.zeros_like(t), (w_gu, b_gu, w_down, b_down, comb.T))
    return y.reshape(bsz, seq, d)


def setup_inputs(seed: int = 0) -> dict:
    key = jax.random.key(seed)
    ks = jax.random.split(key, 20)
    d, hd = D_MODEL, N_HEADS * HEAD_DIM

    def nrm(k, shape, s):
        return jax.random.normal(k, shape, jnp.float32) * s

    return {
        "x": nrm(ks[0], (BATCH, SEQ, d), 1.0),
        "c": nrm(ks[1], (BATCH, d), 1.0),
        "positions": jnp.broadcast_to(jnp.arange(SEQ, dtype=jnp.int32), (BATCH, SEQ)),
        "norm_g": 1.0 + nrm(ks[2], (DEPTH, 2, d), 0.1),
        "ada_w": nrm(ks[3], (DEPTH, d, N_ADA * d), 0.5 * d ** -0.5),
        "ada_b": nrm(ks[4], (DEPTH, N_ADA * d), 0.02),
        "conv_w_in": nrm(ks[5], (N_A_LAYERS, d, 3 * d), d ** -0.5),
        "conv_w": nrm(ks[6], (N_A_LAYERS, CONV_WIDTH, d), CONV_WIDTH ** -0.5),
        "conv_w_out": nrm(ks[7], (N_A_LAYERS, d, d), d ** -0.5),
        "kv_norm_g": 1.0 + nrm(ks[8], (d,), 0.1),
        "w_kv": nrm(ks[9], (d, 2 * hd), d ** -0.5),
        "attn_w_q": nrm(ks[10], (N_B_LAYERS, d, hd), d ** -0.5),
        "attn_w_o": nrm(ks[11], (N_B_LAYERS, hd, d), hd ** -0.5),
        "router_w": nrm(ks[12], (DEPTH, d, N_EXPERTS), d ** -0.5),
        "router_b": nrm(ks[13], (DEPTH, N_EXPERTS), 0.01),
        "moe_w_gu": nrm(ks[14], (DEPTH, N_EXPERTS, d, 2 * D_FF), d ** -0.5),
        "moe_b_gu": nrm(ks[15], (DEPTH, N_EXPERTS, 2 * D_FF), 0.01),
        "moe_w_down": nrm(ks[16], (DEPTH, N_EXPERTS, D_FF, d), D_FF ** -0.5),
        "moe_b_down": nrm(ks[17], (DEPTH, N_EXPERTS, d), 0.01),
        "final_g": 1.0 + nrm(ks[18], (d,), 0.1),
    }


def reference(x, c, positions, norm_g, ada_w, ada_b, conv_w_in, conv_w, conv_w_out,
              kv_norm_g, w_kv, attn_w_q, attn_w_o, router_w, router_b,
              moe_w_gu, moe_b_gu, moe_w_down, moe_b_down, final_g):
    cos, sin = rope_tables(positions)
    c_act = jax.nn.silu(c)
    k_pad = v_pad = k_mean = None
    for layer in range(DEPTH):
        ada = (c_act @ ada_w[layer] + ada_b[layer])[:, None, :]
        sh1, sc1, gt1, sh2, sc2, gt2 = jnp.split(ada, N_ADA, axis=-1)
        h = modulate(x, norm_g[layer, 0], sh1, sc1)
        if layer < N_A_LAYERS:
            mix = short_conv_mixer(h, conv_w_in[layer], conv_w[layer], conv_w_out[layer])
        else:
            if layer == N_A_LAYERS:
                k_pad, v_pad, k_mean = shared_kv(x, kv_norm_g, w_kv, cos, sin)
            j = layer - N_A_LAYERS
            mix = moba_attention(h, attn_w_q[j], attn_w_o[j], k_pad, v_pad, k_mean, cos, sin)
        x = x + gt1 * mix
        h = modulate(x, norm_g[layer, 1], sh2, sc2)
        x = x + gt2 * moe_ffn(h, router_w[layer], router_b[layer], moe_w_gu[layer], moe_b_gu[layer],
                              moe_w_down[layer], moe_b_down[layer])
    return rmsnorm(x, final_g)
```

```python
import functools

import jax
import jax.numpy as jnp
from jax import lax
from jax.experimental import pallas as pl
from jax.experimental.pallas import tpu as pltpu

F32 = jnp.float32
BF16 = jnp.bfloat16
I32 = jnp.int32

D_MODEL = 2048
BATCH = 4
SEQ = 2048
N_TOK = BATCH * SEQ
N_HEADS = 16
HEAD_DIM = 128
ROT_DIM = 32
ROT_HALF = ROT_DIM // 2
ROPE_THETA = 500000.0
MOBA_BLOCK = 256
MOBA_TOPK = 3
N_BLOCKS = SEQ // MOBA_BLOCK
N_EXPERTS = 32
TOP_K = 4
D_FF = D_MODEL
SWIGLU_ALPHA = 1.702
SWIGLU_LIMIT = 7.0
NORM_EPS = 1e-5
NEG_INF = -1e30
N_ADA = 6

LANES = 128
VMEM_LIMIT = 56 * 1024 * 1024

SUB = 256
GROUP_SUBS = 8
N_SLOTS = N_TOK * TOP_K
N_ROWS = N_SLOTS + N_EXPERTS * SUB
N_GROUPS = N_EXPERTS + N_SLOTS // (SUB * GROUP_SUBS)
FF_TILE = 256
N_FF_TILES = D_FF // FF_TILE

NT_DIMS = (((1,), (1,)), ((), ()))


def _cparams(n_axes):
    return pltpu.CompilerParams(dimension_semantics=("arbitrary",) * n_axes,
                                vmem_limit_bytes=VMEM_LIMIT)


def _rms_scale(x):
    return x * lax.rsqrt(jnp.mean(x * x, axis=-1, keepdims=True) + NORM_EPS)


def _modulate(x, g, shift, scale):
    return (_rms_scale(x) * g) * (1 + scale) + shift


def _dot(a, b):
    return jnp.dot(a, b, preferred_element_type=F32)


ADA_TN = 1024


def _ada_kernel(c_ref, w_ref, b_ref, o_ref):
    c = c_ref[...]
    c_act = c * jax.nn.sigmoid(c)
    o_ref[0] = _dot(c_act.astype(BF16), w_ref[0].astype(BF16)) + b_ref[0]


def _ada(c, ada_w, ada_b):
    depth = ada_w.shape[0]
    n_out = ada_w.shape[2]
    c_pad = jnp.zeros((8, D_MODEL), F32).at[:BATCH].set(c)
    return pl.pallas_call(
        _ada_kernel,
        grid=(depth, n_out // ADA_TN),
        in_specs=[
            pl.BlockSpec((8, D_MODEL), lambda l, j: (0, 0)),
            pl.BlockSpec((1, D_MODEL, ADA_TN), lambda l, j: (l, 0, j)),
            pl.BlockSpec((1, 1, ADA_TN), lambda l, j: (l, 0, j)),
        ],
        out_specs=pl.BlockSpec((1, 8, ADA_TN), lambda l, j: (l, 0, j)),
        out_shape=jax.ShapeDtypeStruct((depth, 8, n_out), F32),
        compiler_params=_cparams(2),
        name="ada",
    )(c_pad, ada_w, ada_b.reshape(depth, 1, n_out))


def _route(x1, i, ng_ref, mod_ref, rw_ref, rb_ref,
           h2_ref, idx_ref, wgt_ref, rank_ref, cnt_ref, cnt_scr):
    tm = x1.shape[0]
    h2 = _modulate(x1, ng_ref[1:2, :], mod_ref[0, 3:4, :], mod_ref[0, 4:5, :])
    h2_ref[...] = h2
    logits = jnp.dot(h2, rw_ref[...], preferred_element_type=F32,
                     precision=lax.Precision.HIGHEST) + rb_ref[...]
    lane = lax.broadcasted_iota(I32, (tm, N_EXPERTS), 1)
    vals, idxs = [], []
    rem = logits
    for _ in range(TOP_K):
        m = jnp.max(rem, axis=-1, keepdims=True)
        ik = jnp.min(jnp.where(rem == m, lane, N_EXPERTS), axis=-1, keepdims=True)
        vals.append(m)
        idxs.append(ik)
        rem = jnp.where(lane == ik, -jnp.inf, rem)
    exps = [jnp.exp(v - vals[0]) for v in vals]
    den = exps[0] + exps[1] + exps[2] + exps[3]
    hot = jnp.zeros((tm, N_EXPERTS), F32)
    for ik in idxs:
        hot = jnp.where(lane == ik, 1.0, hot)
    row = lax.broadcasted_iota(I32, (tm, tm), 0)
    col = lax.broadcasted_iota(I32, (tm, tm), 1)
    tri = jnp.where(col < row, 1.0, 0.0).astype(BF16)
    prior = cnt_scr[...]
    rank_mat = _dot(tri, hot.astype(BF16)) + prior
    cnt_new = prior + jnp.sum(hot, axis=0, keepdims=True)
    cnt_scr[...] = cnt_new
    cnt_ref[...] = cnt_new
    lane_o = lax.broadcasted_iota(I32, (tm, LANES), 1)
    idx_o = jnp.zeros((tm, LANES), I32)
    wgt_o = jnp.zeros((tm, LANES), F32)
    rank_o = jnp.zeros((tm, LANES), I32)
    for k in range(TOP_K):
        rk = jnp.sum(jnp.where(lane == idxs[k], rank_mat, 0.0), axis=-1, keepdims=True)
        idx_o = jnp.where(lane_o == k, idxs[k], idx_o)
        wgt_o = jnp.where(lane_o == k, exps[k] / den, wgt_o)
        rank_o = jnp.where(lane_o == k, rk.astype(I32), rank_o)
    idx_ref[...] = idx_o
    wgt_ref[...] = wgt_o
    rank_ref[...] = rank_o


def _route_out_shapes():
    return (
        jax.ShapeDtypeStruct((N_TOK, D_MODEL), F32),
        jax.ShapeDtypeStruct((N_TOK, D_MODEL), F32),
        jax.ShapeDtypeStruct((N_TOK, LANES), I32),
        jax.ShapeDtypeStruct((N_TOK, LANES), F32),
        jax.ShapeDtypeStruct((N_TOK, LANES), I32),
        jax.ShapeDtypeStruct((1, N_EXPERTS), F32),
    )


MIX_TM = 512
MIX_TN = 256
MIX_NC = D_MODEL // MIX_TN


def _mixer0_kernel(x_ref, mod_ref, ng_ref, wb_ref, wc_ref, wu_ref, cw_ref, wo_ref, rw_ref, rb_ref,
                   x1_ref, h2_ref, idx_ref, wgt_ref, rank_ref, cnt_ref,
                   h_scr, vpad_scr, carry_scr, cnt_scr):
    i = pl.program_id(0)
    j = pl.program_id(1)
    tm = MIX_TM

    @pl.when(jnp.logical_and(i == 0, j == 0))
    def _():
        cnt_scr[...] = jnp.zeros_like(cnt_scr)

    @pl.when(j == 0)
    def _():
        h = _modulate(x_ref[...], ng_ref[0:1, :], mod_ref[0, 0:1, :], mod_ref[0, 1:2, :])
        h_scr[...] = h.astype(BF16)

    h = h_scr[...]
    b_gate = _dot(h, wb_ref[...])
    v = _dot(h, wc_ref[...]) * _dot(h, wu_ref[...])
    seq_start = (i % (SEQ // tm)) == 0

    @pl.when(seq_start)
    def _():
        vpad_scr[0:8, :] = jnp.zeros((8, MIX_TN), F32)

    @pl.when(jnp.logical_not(seq_start))
    def _():
        vpad_scr[0:8, :] = carry_scr[j]

    vpad_scr[8:8 + tm, :] = v
    carry_scr[j] = v[tm - 8:tm, :]
    cw = cw_ref[...]
    conv = (cw[0:1, :] * vpad_scr[6:6 + tm, :] + cw[1:2, :] * vpad_scr[7:7 + tm, :]
            + cw[2:3, :] * v)
    contrib = _dot((b_gate * conv).astype(BF16), wo_ref[...])

    @pl.when(j == 0)
    def _():
        x1_ref[...] = contrib

    @pl.when(j > 0)
    def _():
        x1_ref[...] += contrib

    @pl.when(j == MIX_NC - 1)
    def _():
        x1 = x_ref[...] + mod_ref[0, 2:3, :] * x1_ref[...]
        x1_ref[...] = x1
        _route(x1, i, ng_ref, mod_ref, rw_ref, rb_ref,
               h2_ref, idx_ref, wgt_ref, rank_ref, cnt_ref, cnt_scr)


def _mixer0(x, mod, ng, w_in, cw, w_out, rw, rb):
    tm, tn, nc = MIX_TM, MIX_TN, MIX_NC
    row = lambda i, j: (i, 0)
    fixed = lambda i, j: (0, 0)
    return pl.pallas_call(
        _mixer0_kernel,
        grid=(N_TOK // tm, nc),
        in_specs=[
            pl.BlockSpec((tm, D_MODEL), row),
            pl.BlockSpec((1, N_ADA, D_MODEL), lambda i, j: (i * tm // SEQ, 0, 0)),
            pl.BlockSpec((2, D_MODEL), fixed),
            pl.BlockSpec((D_MODEL, tn), lambda i, j: (0, j)),
            pl.BlockSpec((D_MODEL, tn), lambda i, j: (0, nc + j)),
            pl.BlockSpec((D_MODEL, tn), lambda i, j: (0, 2 * nc + j)),
            pl.BlockSpec((3, tn), lambda i, j: (0, j)),
            pl.BlockSpec((tn, D_MODEL), lambda i, j: (j, 0)),
            pl.BlockSpec((D_MODEL, N_EXPERTS), fixed),
            pl.BlockSpec((1, N_EXPERTS), fixed),
        ],
        out_specs=(
            pl.BlockSpec((tm, D_MODEL), row),
            pl.BlockSpec((tm, D_MODEL), row),
            pl.BlockSpec((tm, LANES), row),
            pl.BlockSpec((tm, LANES), row),
            pl.BlockSpec((tm, LANES), row),
            pl.BlockSpec((1, N_EXPERTS), fixed),
        ),
        out_shape=_route_out_shapes(),
        scratch_shapes=[
            pltpu.VMEM((tm, D_MODEL), BF16),
            pltpu.VMEM((tm + 8, tn), F32),
            pltpu.VMEM((nc, 8, tn), F32),
            pltpu.VMEM((1, N_EXPERTS), F32),
        ],
        compiler_params=_cparams(2),
        name="mixer0",
    )(x, mod, ng, w_in, w_in, w_in, cw, w_out, rw, rb)


def _moe_kernel(ge_ref, gr_ref, gn_ref, tok_ref, tail_ref,
                h_hbm, wg_ref, wu_ref, wd_ref, bg_ref, bu_ref, bd_ref,
                out_hbm,
                xg, stage, acc, wg_bf, wu_bf, wd_bf, gsem, osem):
    g = pl.program_id(0)
    j = pl.program_id(1)
    ns = gn_ref[g]
    row0 = gr_ref[g]

    def issue_rows(s, slot):
        base = row0 + s * SUB

        def body(r, carry):
            tok = tok_ref[base + r]
            pltpu.make_async_copy(h_hbm.at[pl.ds(tok, 1), :],
                                  stage.at[slot, pl.ds(r, 1), :], gsem.at[slot]).start()
            return carry

        lax.fori_loop(0, SUB, body, 0)

    def wait_rows(slot):
        def body(r, carry):
            pltpu.make_async_copy(h_hbm.at[pl.ds(0, 1), :],
                                  stage.at[slot, pl.ds(r, 1), :], gsem.at[slot]).wait()
            return carry

        lax.fori_loop(0, SUB, body, 0)

    @pl.when(jnp.logical_and(j == 0, ns > 0))
    def _():
        issue_rows(0, 0)

        def body(s, carry):
            slot = s % 2

            @pl.when(s + 1 < ns)
            def _():
                issue_rows(s + 1, 1 - slot)

            wait_rows(slot)
            xg[s] = stage[slot].astype(BF16)
            return carry

        lax.fori_loop(0, ns, body, 0)

    @pl.when(ns > 0)
    def _():
        wg_bf[...] = wg_ref[0].astype(BF16)
        wu_bf[...] = wu_ref[0].astype(BF16)
        wd_bf[...] = wd_ref[0].astype(BF16)

        def body(s, carry):
            xs = xg[s]
            gate = jnp.minimum(_dot(xs, wg_bf[...]) + bg_ref[0], SWIGLU_LIMIT)
            up = jnp.clip(_dot(xs, wu_bf[...]) + bu_ref[0], -SWIGLU_LIMIT, SWIGLU_LIMIT)
            glu = gate * jax.nn.sigmoid(gate * SWIGLU_ALPHA)
            contrib = _dot(((up + 1) * glu).astype(BF16), wd_bf[...])

            @pl.when(j == 0)
            def _():
                acc[s] = contrib + bd_ref[0]

            @pl.when(j > 0)
            def _():
                acc[s] += contrib

            return carry

        lax.fori_loop(0, ns, body, 0)

    @pl.when(jnp.logical_and(j == N_FF_TILES - 1, ns > 0))
    def _():
        def out_copy(s):
            dst = pl.multiple_of(row0 + s * SUB, SUB)
            return pltpu.make_async_copy(acc.at[s], out_hbm.at[pl.ds(dst, SUB), :], osem)

        def start(s, carry):
            out_copy(s).start()
            return carry

        def wait(s, carry):
            out_copy(s).wait()
            return carry

        lax.fori_loop(0, ns, start, 0)
        lax.fori_loop(0, ns, wait, 0)

    @pl.when(jnp.logical_and(g == N_GROUPS - 1, j == N_FF_TILES - 1))
    def _():
        stage[0] = jnp.zeros((SUB, D_MODEL), F32)
        tail0 = tail_ref[0]
        n_tail = (N_ROWS - tail0) // SUB

        def tail_copy(s):
            dst = pl.multiple_of(tail0 + s * SUB, SUB)
            return pltpu.make_async_copy(stage.at[0], out_hbm.at[pl.ds(dst, SUB), :], osem)

        def start(s, carry):
            tail_copy(s).start()
            return carry

        def wait(s, carry):
            tail_copy(s).wait()
            return carry

        lax.fori_loop(0, n_tail, start, 0)
        lax.fori_loop(0, n_tail, wait, 0)


def _moe(layer, h2, plan, w_gu, b_gu, w_down, b_down):
    ge, gr, gn, tok, tail = plan
    nj = N_FF_TILES

    def jj(g, j, gn_ref):
        return jnp.where(gn_ref[g] > 0, j, nj - 1)

    sq = pl.Squeezed()
    grid_spec = pltpu.PrefetchScalarGridSpec(
        num_scalar_prefetch=5,
        grid=(N_GROUPS, nj),
        in_specs=[
            pl.BlockSpec(memory_space=pl.ANY),
            pl.BlockSpec((sq, 1, D_MODEL, FF_TILE),
                         lambda g, j, ge, gr, gn, *_: (layer, ge[g], 0, jj(g, j, gn))),
            pl.BlockSpec((sq, 1, D_MODEL, FF_TILE),
                         lambda g, j, ge, gr, gn, *_: (layer, ge[g], 0, nj + jj(g, j, gn))),
            pl.BlockSpec((sq, 1, FF_TILE, D_MODEL),
                         lambda g, j, ge, gr, gn, *_: (layer, ge[g], jj(g, j, gn), 0)),
            pl.BlockSpec((sq, 1, 1, FF_TILE),
                         lambda g, j, ge, gr, gn, *_: (layer, ge[g], 0, jj(g, j, gn))),
            pl.BlockSpec((sq, 1, 1, FF_TILE),
                         lambda g, j, ge, gr, gn, *_: (layer, ge[g], 0, nj + jj(g, j, gn))),
            pl.BlockSpec((sq, 1, 1, D_MODEL),
                         lambda g, j, ge, gr, gn, *_: (layer, ge[g], 0, 0)),
        ],
        out_specs=pl.BlockSpec(memory_space=pl.ANY),
        scratch_shapes=[
            pltpu.VMEM((GROUP_SUBS, SUB, D_MODEL), BF16),
            pltpu.VMEM((2, SUB, D_MODEL), F32),
            pltpu.VMEM((GROUP_SUBS, SUB, D_MODEL), F32),
            pltpu.VMEM((D_MODEL, FF_TILE), BF16),
            pltpu.VMEM((D_MODEL, FF_TILE), BF16),
            pltpu.VMEM((FF_TILE, D_MODEL), BF16),
            pltpu.SemaphoreType.DMA((2,)),
            pltpu.SemaphoreType.DMA(()),
        ],
    )
    return pl.pallas_call(
        _moe_kernel,
        grid_spec=grid_spec,
        out_shape=jax.ShapeDtypeStruct((N_ROWS, D_MODEL), F32),
        compiler_params=_cparams(2),
        name="moe",
    )(ge, gr, gn, tok, tail, h2, w_gu, w_gu, w_down, b_gu, b_gu, b_down)


def _plan(idx, rank, cnt):
    idx = idx[:, :TOP_K]
    rank = rank[:, :TOP_K]
    counts = cnt.reshape(N_EXPERTS).astype(I32)
    ntile = (counts + SUB - 1) // SUB
    start = (jnp.cumsum(ntile) - ntile) * SUB
    onehot = idx[..., None] == jnp.arange(N_EXPERTS, dtype=I32)
    pos = jnp.sum(jnp.where(onehot, start, 0), axis=-1) + rank
    pos = pos.reshape(N_SLOTS).astype(I32)
    tok = jnp.zeros((N_ROWS,), I32).at[pos].set(
        jnp.repeat(jnp.arange(N_TOK, dtype=I32), TOP_K), unique_indices=True)
    ngrp = (ntile + GROUP_SUBS - 1) // GROUP_SUBS
    gcum = jnp.cumsum(ngrp)
    total = gcum[-1]
    g = jnp.arange(N_GROUPS, dtype=I32)
    e_g = jnp.minimum(jnp.searchsorted(gcum, g, side="right").astype(I32), N_EXPERTS - 1)
    local = g - (gcum - ngrp)[e_g]
    valid = g < total
    nsub = jnp.where(valid, jnp.minimum(GROUP_SUBS, ntile[e_g] - local * GROUP_SUBS), 0)
    row0 = jnp.where(valid, start[e_g] + local * GROUP_SUBS * SUB, 0)
    e_last = e_g[jnp.maximum(total - 1, 0)]
    e_g = jnp.where(valid, e_g, e_last)
    tail = (jnp.sum(ntile) * SUB).astype(I32).reshape(1)
    return pos, (e_g.astype(I32), row0.astype(I32), nsub.astype(I32), tok, tail)


COMB_TM = 256


def _combine_kernel(pos_ref, x1_ref, mod_ref, wgt_ref, y_hbm, fg_ref, o_ref, buf, sem, *, final):
    i = pl.program_id(0)
    n = pl.num_programs(0)
    tm = COMB_TM

    def issue(tile, slot):
        base = tile * (tm * TOP_K)

        def body(r, carry):
            for k in range(TOP_K):
                p = pos_ref[base + r * TOP_K + k]
                pltpu.make_async_copy(y_hbm.at[pl.ds(p, 1), :],
                                      buf.at[slot, k, pl.ds(r, 1), :], sem.at[slot]).start()
            return carry

        lax.fori_loop(0, tm, body, 0)

    def wait(slot):
        def body(r, carry):
            for k in range(TOP_K):
                pltpu.make_async_copy(y_hbm.at[pl.ds(0, 1), :],
                                      buf.at[slot, k, pl.ds(r, 1), :], sem.at[slot]).wait()
            return carry

        lax.fori_loop(0, tm, body, 0)

    @pl.when(i == 0)
    def _():
        issue(0, 0)

    @pl.when(i + 1 < n)
    def _():
        issue(i + 1, (i + 1) % 2)

    slot = i % 2
    wait(slot)
    w = wgt_ref[...]
    y = w[:, 0:1] * buf[slot, 0]
    for k in range(1, TOP_K):
        y = y + w[:, k:k + 1] * buf[slot, k]
    x2 = x1_ref[...] + mod_ref[0, 5:6, :] * y
    if final:
        o_ref[...] = _rms_scale(x2) * fg_ref[...]
    else:
        o_ref[...] = x2


def _combine(pos, x1, mod, wgt, y_sorted, final_g, final):
    tm = COMB_TM
    grid_spec = pltpu.PrefetchScalarGridSpec(
        num_scalar_prefetch=1,
        grid=(N_TOK // tm,),
        in_specs=[
            pl.BlockSpec((tm, D_MODEL), lambda i, pos: (i, 0)),
            pl.BlockSpec((1, N_ADA, D_MODEL), lambda i, pos: (i * tm // SEQ, 0, 0)),
            pl.BlockSpec((tm, LANES), lambda i, pos: (i, 0)),
            pl.BlockSpec(memory_space=pl.ANY),
            pl.BlockSpec((1, D_MODEL), lambda i, pos: (0, 0)),
        ],
        out_specs=pl.BlockSpec((tm, D_MODEL), lambda i, pos: (i, 0)),
        scratch_shapes=[
            pltpu.VMEM((2, TOP_K, tm, D_MODEL), F32),
            pltpu.SemaphoreType.DMA((2,)),
        ],
    )
    return pl.pallas_call(
        functools.partial(_combine_kernel, final=final),
        grid_spec=grid_spec,
        out_shape=jax.ShapeDtypeStruct((N_TOK, D_MODEL), F32),
        compiler_params=_cparams(1),
        name="combine_final" if final else "combine",
    )(pos, x1, mod, wgt, y_sorted, final_g.reshape(1, D_MODEL))


QKV_TM = 512
QKV_TN = 1024
QKV_NC = D_MODEL // QKV_TN


def _qkv_kernel(x_ref, mod_ref, ng_ref, kvg_ref, w_ref, c_ref, s1_ref, s2_ref, o_ref, h_scr):
    j = pl.program_id(1)
    which = j // QKV_NC

    @pl.when(j == 0)
    def _():
        xn = _rms_scale(x_ref[...])
        hq = (xn * ng_ref[0:1, :]) * (1 + mod_ref[0, 1:2, :]) + mod_ref[0, 0:1, :]
        h_scr[0] = hq.astype(BF16)
        h_scr[1] = (xn * kvg_ref[...]).astype(BF16)

    z = _dot(h_scr[jnp.minimum(which, 1)], w_ref[...])

    @pl.when(which < 2)
    def _():
        scale = jnp.where(which == 0, HEAD_DIM ** -0.5, 1.0).astype(F32)
        cos = c_ref[...]
        s1 = s1_ref[...]
        s2 = s2_ref[...]
        for hh in range(QKV_TN // HEAD_DIM):
            sl = slice(hh * HEAD_DIM, (hh + 1) * HEAD_DIM)
            zz = z[:, sl]
            rot = (zz * cos + pltpu.roll(zz, HEAD_DIM - ROT_HALF, 1) * s1
                   + pltpu.roll(zz, ROT_HALF, 1) * s2)
            o_ref[:, sl] = (rot * scale).astype(BF16)

    @pl.when(which == 2)
    def _():
        o_ref[...] = z.astype(BF16)


def _qkv(x2, mod, ng, kvg, w_qkv, cos_t, sin1_t, sin2_t):
    tm, tn = QKV_TM, QKV_TN
    row = lambda i, j: (i, 0)
    fixed = lambda i, j: (0, 0)
    return pl.pallas_call(
        _qkv_kernel,
        grid=(N_TOK // tm, 3 * QKV_NC),
        in_specs=[
            pl.BlockSpec((tm, D_MODEL), row),
            pl.BlockSpec((1, N_ADA, D_MODEL), lambda i, j: (i * tm // SEQ, 0, 0)),
            pl.BlockSpec((2, D_MODEL), fixed),
            pl.BlockSpec((1, D_MODEL), fixed),
            pl.BlockSpec((D_MODEL, tn), lambda i, j: (0, j)),
            pl.BlockSpec((tm, HEAD_DIM), row),
            pl.BlockSpec((tm, HEAD_DIM), row),
            pl.BlockSpec((tm, HEAD_DIM), row),
        ],
        out_specs=pl.BlockSpec((tm, tn), lambda i, j: (i, j)),
        out_shape=jax.ShapeDtypeStruct((N_TOK, 3 * D_MODEL), BF16),
        scratch_shapes=[pltpu.VMEM((2, tm, D_MODEL), BF16)],
        compiler_params=_cparams(2),
        name="qkv",
    )(x2, mod, ng, kvg.reshape(1, D_MODEL), w_qkv, cos_t, sin1_t, sin2_t)


def _attn_kernel(q_ref, k_ref, v_ref, o_ref, km_scr):
    blk = MOBA_BLOCK
    km_scr[...] = jnp.zeros_like(km_scr)
    for jb in range(N_BLOCKS):
        kb = k_ref[jb * blk:(jb + 1) * blk, :].astype(F32)
        km_scr[jb:jb + 1, :] = jnp.mean(kb, axis=0, keepdims=True)
    km = km_scr[...].astype(BF16)
    lane = lax.broadcasted_iota(I32, (blk, LANES), 1)
    row = lax.broadcasted_iota(I32, (blk, blk), 0)
    col = lax.broadcasted_iota(I32, (blk, blk), 1)

    def q_block(qi, carry):
        q0 = pl.multiple_of(qi * blk, blk)
        q = q_ref[pl.ds(q0, blk), :]
        gate = lax.dot_general(q, km, NT_DIMS, preferred_element_type=F32)
        past = lane < qi
        rem = jnp.where(past, gate, NEG_INF)
        sel = jnp.zeros((blk, LANES), F32)
        for _ in range(MOBA_TOPK):
            m = jnp.max(rem, axis=-1, keepdims=True)
            first = jnp.min(jnp.where(rem == m, lane, LANES), axis=-1, keepdims=True)
            hit = lane == first
            sel = jnp.where(jnp.logical_and(hit, past), 1.0, sel)
            rem = jnp.where(hit, -jnp.inf, rem)
        s = lax.dot_general(q, k_ref[pl.ds(q0, blk), :], NT_DIMS, preferred_element_type=F32)
        s = jnp.where(col <= row, s, NEG_INF)
        m = jnp.max(s, axis=-1, keepdims=True)
        p = jnp.exp(s - m)
        l = jnp.sum(p, axis=-1, keepdims=True)
        acc = _dot(p.astype(BF16), v_ref[pl.ds(q0, blk), :])

        def k_block(jb, c):
            m, l, acc = c
            k0 = pl.multiple_of(jb * blk, blk)
            s = lax.dot_general(q, k_ref[pl.ds(k0, blk), :], NT_DIMS, preferred_element_type=F32)
            keep = jnp.sum(jnp.where(lane == jb, sel, 0.0), axis=-1, keepdims=True) > 0.5
            s = jnp.where(keep, s, NEG_INF)
            m_new = jnp.maximum(m, jnp.max(s, axis=-1, keepdims=True))
            alpha = jnp.exp(m - m_new)
            p = jnp.exp(s - m_new)
            l = alpha * l + jnp.sum(p, axis=-1, keepdims=True)
            acc = alpha * acc + _dot(p.astype(BF16), v_ref[pl.ds(k0, blk), :])
            return m_new, l, acc

        m, l, acc = lax.fori_loop(0, qi, k_block, (m, l, acc))
        o_ref[pl.ds(q0, blk), :] = (acc / l).astype(BF16)
        return carry

    lax.fori_loop(0, N_BLOCKS, q_block, 0)


def _attention(qkv):
    return pl.pallas_call(
        _attn_kernel,
        grid=(BATCH, N_HEADS),
        in_specs=[
            pl.BlockSpec((SEQ, HEAD_DIM), lambda b, h: (b, h)),
            pl.BlockSpec((SEQ, HEAD_DIM), lambda b, h: (b, N_HEADS + h)),
            pl.BlockSpec((SEQ, HEAD_DIM), lambda b, h: (b, 2 * N_HEADS + h)),
        ],
        out_specs=pl.BlockSpec((SEQ, HEAD_DIM), lambda b, h: (b, h)),
        out_shape=jax.ShapeDtypeStruct((N_TOK, D_MODEL), BF16),
        scratch_shapes=[pltpu.VMEM((LANES, HEAD_DIM), F32)],
        compiler_params=_cparams(2),
        name="moba_attention",
    )(qkv, qkv, qkv)


OPROJ_TM = 256


def _oproj_kernel(a_ref, x_ref, mod_ref, ng_ref, w_ref, rw_ref, rb_ref,
                  x1_ref, h2_ref, idx_ref, wgt_ref, rank_ref, cnt_ref, cnt_scr):
    i = pl.program_id(0)

    @pl.when(i == 0)
    def _():
        cnt_scr[...] = jnp.zeros_like(cnt_scr)

    x1 = x_ref[...] + mod_ref[0, 2:3, :] * _dot(a_ref[...], w_ref[...])
    x1_ref[...] = x1
    _route(x1, i, ng_ref, mod_ref, rw_ref, rb_ref,
           h2_ref, idx_ref, wgt_ref, rank_ref, cnt_ref, cnt_scr)


def _oproj(attn, x2, mod, ng, w_o, rw, rb):
    tm = OPROJ_TM
    row = lambda i: (i, 0)
    fixed = lambda i: (0, 0)
    return pl.pallas_call(
        _oproj_kernel,
        grid=(N_TOK // tm,),
        in_specs=[
            pl.BlockSpec((tm, D_MODEL), row),
            pl.BlockSpec((tm, D_MODEL), row),
            pl.BlockSpec((1, N_ADA, D_MODEL), lambda i: (i * tm // SEQ, 0, 0)),
            pl.BlockSpec((2, D_MODEL), fixed),
            pl.BlockSpec((D_MODEL, D_MODEL), fixed),
            pl.BlockSpec((D_MODEL, N_EXPERTS), fixed),
            pl.BlockSpec((1, N_EXPERTS), fixed),
        ],
        out_specs=(
            pl.BlockSpec((tm, D_MODEL), row),
            pl.BlockSpec((tm, D_MODEL), row),
            pl.BlockSpec((tm, LANES), row),
            pl.BlockSpec((tm, LANES), row),
            pl.BlockSpec((tm, LANES), row),
            pl.BlockSpec((1, N_EXPERTS), fixed),
        ),
        out_shape=_route_out_shapes(),
        scratch_shapes=[pltpu.VMEM((1, N_EXPERTS), F32)],
        compiler_params=_cparams(1),
        name="oproj",
    )(attn, x2, mod, ng, w_o, rw, rb)


def _rope_tables(positions):
    inv = jnp.float32(ROPE_THETA) ** (-jnp.arange(0, ROT_DIM, 2, dtype=F32) / ROT_DIM)
    ang = positions.astype(F32).reshape(N_TOK, 1) * inv
    cos, sin = jnp.cos(ang), jnp.sin(ang)
    rest = HEAD_DIM - ROT_DIM
    cos_t = jnp.concatenate([cos, cos, jnp.ones((N_TOK, rest), F32)], axis=1)
    sin1_t = jnp.concatenate([-sin, jnp.zeros((N_TOK, HEAD_DIM - ROT_HALF), F32)], axis=1)
    sin2_t = jnp.concatenate([jnp.zeros((N_TOK, ROT_HALF), F32), sin,
                              jnp.zeros((N_TOK, rest), F32)], axis=1)
    return cos_t, sin1_t, sin2_t


def _moe_block(layer, x1, h2, idx, wgt, rank, cnt, mod, moe_w_gu, moe_b_gu, moe_w_down,
               moe_b_down, final_g, final):
    pos, plan = _plan(idx, rank, cnt)
    depth = moe_w_gu.shape[0]
    y_sorted = _moe(layer, h2, plan, moe_w_gu,
                    moe_b_gu.reshape(depth, N_EXPERTS, 1, 2 * D_FF), moe_w_down,
                    moe_b_down.reshape(depth, N_EXPERTS, 1, D_MODEL))
    return _combine(pos, x1, mod, wgt, y_sorted, final_g, final)


def kernel(x, c, positions, norm_g, ada_w, ada_b, conv_w_in, conv_w, conv_w_out, kv_norm_g, w_kv,
           attn_w_q, attn_w_o, router_w, router_b, moe_w_gu, moe_b_gu, moe_w_down, moe_b_down,
           final_g):
    xf = x.reshape(N_TOK, D_MODEL)
    ada = _ada(c, ada_w, ada_b)
    mod = ada[:, :BATCH].reshape(2, BATCH, N_ADA, D_MODEL)
    cos_t, sin1_t, sin2_t = _rope_tables(positions)

    x1, h2, idx, wgt, rank, cnt = _mixer0(
        xf, mod[0], norm_g[0], conv_w_in[0].astype(BF16), conv_w[0], conv_w_out[0].astype(BF16),
        router_w[0], router_b[0].reshape(1, N_EXPERTS))
    x2 = _moe_block(0, x1, h2, idx, wgt, rank, cnt, mod[0], moe_w_gu, moe_b_gu, moe_w_down,
                    moe_b_down, final_g, False)

    w_qkv = jnp.concatenate([attn_w_q[0], w_kv], axis=1).astype(BF16)
    qkv = _qkv(x2, mod[1], norm_g[1], kv_norm_g, w_qkv, cos_t, sin1_t, sin2_t)
    attn = _attention(qkv)
    x3, h2b, idx, wgt, rank, cnt = _oproj(
        attn, x2, mod[1], norm_g[1], attn_w_o[0].astype(BF16), router_w[1],
        router_b[1].reshape(1, N_EXPERTS))
    out = _moe_block(1, x3, h2b, idx, wgt, rank, cnt, mod[1], moe_w_gu, moe_b_gu, moe_w_down,
                     moe_b_down, final_g, True)
    return out.reshape(BATCH, SEQ, D_MODEL)
```

```python
import functools

import jax
import jax.numpy as jnp
from jax import lax
from jax.experimental import pallas as pl
from jax.experimental.pallas import tpu as pltpu

F32 = jnp.float32
BF16 = jnp.bfloat16
I32 = jnp.int32

D_MODEL = 2048
BATCH = 4
SEQ = 2048
N_TOK = BATCH * SEQ
N_HEADS = 16
HEAD_DIM = 128
ROT_DIM = 32
ROT_HALF = ROT_DIM // 2
ROPE_THETA = 500000.0
MOBA_BLOCK = 256
MOBA_TOPK = 3
N_BLOCKS = SEQ // MOBA_BLOCK
N_EXPERTS = 32
TOP_K = 4
D_FF = D_MODEL
SWIGLU_ALPHA = 1.702
SWIGLU_LIMIT = 7.0
NORM_EPS = 1e-5
NEG_INF = -1e30
N_ADA = 6

LANES = 128
VMEM_LIMIT = 56 * 1024 * 1024

SUB = 256
GROUP_SUBS = 8
N_SLOTS = N_TOK * TOP_K
N_ROWS = N_SLOTS + N_EXPERTS * SUB
N_GROUPS = N_EXPERTS + N_SLOTS // (SUB * GROUP_SUBS)
DMA_UNROLL = 8
FF_TILE = 256
N_FF_TILES = D_FF // FF_TILE

NT_DIMS = (((1,), (1,)), ((), ()))


def _cparams(n_axes):
    return pltpu.CompilerParams(dimension_semantics=("arbitrary",) * n_axes,
                                vmem_limit_bytes=VMEM_LIMIT)


def _rms_scale(x):
    return x * lax.rsqrt(jnp.mean(x * x, axis=-1, keepdims=True) + NORM_EPS)


def _modulate(x, g, shift, scale):
    return (_rms_scale(x) * g) * (1 + scale) + shift


def _dot(a, b):
    return jnp.dot(a, b, preferred_element_type=F32)


ADA_TN = 1024


def _ada_kernel(c_ref, w_ref, b_ref, o_ref):
    c = c_ref[...]
    c_act = c * jax.nn.sigmoid(c)
    o_ref[0] = _dot(c_act.astype(BF16), w_ref[0].astype(BF16)) + b_ref[0]


def _ada(c, ada_w, ada_b):
    depth = ada_w.shape[0]
    n_out = ada_w.shape[2]
    c_pad = jnp.zeros((8, D_MODEL), F32).at[:BATCH].set(c)
    return pl.pallas_call(
        _ada_kernel,
        grid=(depth, n_out // ADA_TN),
        in_specs=[
            pl.BlockSpec((8, D_MODEL), lambda l, j: (0, 0)),
            pl.BlockSpec((1, D_MODEL, ADA_TN), lambda l, j: (l, 0, j)),
            pl.BlockSpec((1, 1, ADA_TN), lambda l, j: (l, 0, j)),
        ],
        out_specs=pl.BlockSpec((1, 8, ADA_TN), lambda l, j: (l, 0, j)),
        out_shape=jax.ShapeDtypeStruct((depth, 8, n_out), F32),
        compiler_params=_cparams(2),
        name="ada",
    )(c_pad, ada_w, ada_b.reshape(depth, 1, n_out))


def _route(x1, i, ng_ref, mod_ref, rw_ref, rb_ref,
           h2_ref, idx_ref, wgt_ref, rank_ref, cnt_ref, cnt_scr):
    tm = x1.shape[0]
    h2 = _modulate(x1, ng_ref[1:2, :], mod_ref[0, 3:4, :], mod_ref[0, 4:5, :])
    h2_ref[...] = h2
    h_hi = h2.astype(BF16)
    h_lo = (h2 - h_hi.astype(F32)).astype(BF16)
    r_hi = _dot(h_hi, rw_ref[...])
    r_lo = _dot(h_lo, rw_ref[...])
    logits = (r_hi[:, :N_EXPERTS] + r_hi[:, N_EXPERTS:] + r_lo[:, :N_EXPERTS]) + rb_ref[...]
    lane = lax.broadcasted_iota(I32, (tm, N_EXPERTS), 1)
    vals, idxs = [], []
    rem = logits
    for _ in range(TOP_K):
        m = jnp.max(rem, axis=-1, keepdims=True)
        ik = jnp.min(jnp.where(rem == m, lane, N_EXPERTS), axis=-1, keepdims=True)
        vals.append(m)
        idxs.append(ik)
        rem = jnp.where(lane == ik, -jnp.inf, rem)
    exps = [jnp.exp(v - vals[0]) for v in vals]
    den = exps[0] + exps[1] + exps[2] + exps[3]
    hot = jnp.zeros((tm, N_EXPERTS), F32)
    for ik in idxs:
        hot = jnp.where(lane == ik, 1.0, hot)
    row = lax.broadcasted_iota(I32, (tm, tm), 0)
    col = lax.broadcasted_iota(I32, (tm, tm), 1)
    tri = jnp.where(col < row, 1.0, 0.0).astype(BF16)
    prior = cnt_scr[...]
    rank_mat = _dot(tri, hot.astype(BF16)) + prior
    cnt_new = prior + jnp.sum(hot, axis=0, keepdims=True)
    cnt_scr[...] = cnt_new
    cnt_ref[...] = cnt_new
    lane_o = lax.broadcasted_iota(I32, (tm, LANES), 1)
    idx_o = jnp.zeros((tm, LANES), I32)
    wgt_o = jnp.zeros((tm, LANES), F32)
    rank_o = jnp.zeros((tm, LANES), I32)
    for k in range(TOP_K):
        rk = jnp.sum(jnp.where(lane == idxs[k], rank_mat, 0.0), axis=-1, keepdims=True)
        idx_o = jnp.where(lane_o == k, idxs[k], idx_o)
        wgt_o = jnp.where(lane_o == k, exps[k] / den, wgt_o)
        rank_o = jnp.where(lane_o == k, rk.astype(I32), rank_o)
    idx_ref[...] = idx_o
    wgt_ref[...] = wgt_o
    rank_ref[...] = rank_o


def _route_out_shapes():
    return (
        jax.ShapeDtypeStruct((N_TOK, D_MODEL), F32),
        jax.ShapeDtypeStruct((N_TOK, D_MODEL), F32),
        jax.ShapeDtypeStruct((N_TOK, LANES), I32),
        jax.ShapeDtypeStruct((N_TOK, LANES), F32),
        jax.ShapeDtypeStruct((N_TOK, LANES), I32),
        jax.ShapeDtypeStruct((1, N_EXPERTS), F32),
    )


MIX_TM = 512
MIX_TN = 512
MIX_NC = D_MODEL // MIX_TN


def _mixer0_kernel(x_ref, mod_ref, ng_ref, wb_ref, wc_ref, wu_ref, cw_ref, wo_ref, rw_ref, rb_ref,
                   x1_ref, h2_ref, idx_ref, wgt_ref, rank_ref, cnt_ref,
                   h_scr, vpad_scr, carry_scr, cnt_scr):
    i = pl.program_id(0)
    j = pl.program_id(1)
    tm = MIX_TM

    @pl.when(jnp.logical_and(i == 0, j == 0))
    def _():
        cnt_scr[...] = jnp.zeros_like(cnt_scr)
        carry_scr[...] = jnp.zeros_like(carry_scr)

    @pl.when(j == 0)
    def _():
        h = _modulate(x_ref[...], ng_ref[0:1, :], mod_ref[0, 0:1, :], mod_ref[0, 1:2, :])
        h_scr[...] = h.astype(BF16)
        x1_ref[...] = jnp.zeros_like(x1_ref)

    h = h_scr[...]
    b_gate = _dot(h, wb_ref[...])
    v = _dot(h, wc_ref[...]) * _dot(h, wu_ref[...])
    seq_start = (i % (SEQ // tm)) == 0
    vpad_scr[0:8, :] = jnp.where(seq_start, 0.0, carry_scr[j])
    vpad_scr[8:8 + tm, :] = v
    carry_scr[j] = v[tm - 8:tm, :]
    cw = cw_ref[...]
    conv = (cw[0:1, :] * vpad_scr[6:6 + tm, :] + cw[1:2, :] * vpad_scr[7:7 + tm, :]
            + cw[2:3, :] * v)
    x1_ref[...] += _dot((b_gate * conv).astype(BF16), wo_ref[...])

    @pl.when(j == MIX_NC - 1)
    def _():
        x1 = x_ref[...] + mod_ref[0, 2:3, :] * x1_ref[...]
        x1_ref[...] = x1
        _route(x1, i, ng_ref, mod_ref, rw_ref, rb_ref,
               h2_ref, idx_ref, wgt_ref, rank_ref, cnt_ref, cnt_scr)


def _mixer0(x, mod, ng, w_in, cw, w_out, rw, rb):
    tm, tn, nc = MIX_TM, MIX_TN, MIX_NC
    row = lambda i, j: (i, 0)
    fixed = lambda i, j: (0, 0)
    return pl.pallas_call(
        _mixer0_kernel,
        grid=(N_TOK // tm, nc),
        in_specs=[
            pl.BlockSpec((tm, D_MODEL), row),
            pl.BlockSpec((1, N_ADA, D_MODEL), lambda i, j: (i * tm // SEQ, 0, 0)),
            pl.BlockSpec((2, D_MODEL), fixed),
            pl.BlockSpec((D_MODEL, tn), lambda i, j: (0, j)),
            pl.BlockSpec((D_MODEL, tn), lambda i, j: (0, nc + j)),
            pl.BlockSpec((D_MODEL, tn), lambda i, j: (0, 2 * nc + j)),
            pl.BlockSpec((3, tn), lambda i, j: (0, j)),
            pl.BlockSpec((tn, D_MODEL), lambda i, j: (j, 0)),
            pl.BlockSpec((D_MODEL, 2 * N_EXPERTS), fixed),
            pl.BlockSpec((1, N_EXPERTS), fixed),
        ],
        out_specs=(
            pl.BlockSpec((tm, D_MODEL), row),
            pl.BlockSpec((tm, D_MODEL), row),
            pl.BlockSpec((tm, LANES), row),
            pl.BlockSpec((tm, LANES), row),
            pl.BlockSpec((tm, LANES), row),
            pl.BlockSpec((1, N_EXPERTS), fixed),
        ),
        out_shape=_route_out_shapes(),
        scratch_shapes=[
            pltpu.VMEM((tm, D_MODEL), BF16),
            pltpu.VMEM((tm + 8, tn), F32),
            pltpu.VMEM((nc, 8, tn), F32),
            pltpu.VMEM((1, N_EXPERTS), F32),
        ],
        compiler_params=_cparams(2),
        name="mixer0",
    )(x, mod, ng, w_in, w_in, w_in, cw, w_out, rw, rb)


def _moe_kernel(ge_ref, gr_ref, gn_ref, tok_ref, tail_ref,
                h_hbm, wg_ref, wu_ref, wd_ref, bg_ref, bu_ref, bd_ref,
                out_hbm,
                xg, stage, acc, wg_bf, wu_bf, wd_bf, gsem, osem):
    g = pl.program_id(0)
    j = pl.program_id(1)
    ns = gn_ref[g]
    row0 = gr_ref[g]

    def row_copy(tok, slot, r):
        return pltpu.make_async_copy(h_hbm.at[pl.ds(tok, 1), :],
                                     stage.at[slot, pl.ds(r, 1), :], gsem.at[slot])

    def issue_rows(s, slot):
        base = row0 + s * SUB

        def body(c, carry):
            for u in range(DMA_UNROLL):
                r = c * DMA_UNROLL + u
                row_copy(tok_ref[base + r], slot, r).start()
            return carry

        lax.fori_loop(0, SUB // DMA_UNROLL, body, 0)

    def wait_rows(slot):
        def body(c, carry):
            for u in range(DMA_UNROLL):
                row_copy(0, slot, c * DMA_UNROLL + u).wait()
            return carry

        lax.fori_loop(0, SUB // DMA_UNROLL, body, 0)

    def sub_rows(s):
        return pl.ds(pl.multiple_of(s * SUB, SUB), SUB)

    @pl.when(jnp.logical_and(j == 0, ns > 0))
    def _():
        issue_rows(0, 0)
        bias = jnp.broadcast_to(bd_ref[0], (SUB, D_MODEL))

        def body(s, carry):
            slot = s % 2

            @pl.when(s + 1 < ns)
            def _():
                issue_rows(s + 1, 1 - slot)

            acc[sub_rows(s), :] = bias
            wait_rows(slot)
            xg[sub_rows(s), :] = stage[slot].astype(BF16)
            return carry

        lax.fori_loop(0, ns, body, 0)

    @pl.when(ns > 0)
    def _():
        wg_bf[...] = wg_ref[0].astype(BF16)
        wu_bf[...] = wu_ref[0].astype(BF16)
        wd_bf[...] = wd_ref[0].astype(BF16)

        def chunk(r0, m):
            rows = pl.ds(pl.multiple_of(r0, SUB), m)
            xs = xg[rows, :]
            gate = jnp.minimum(_dot(xs, wg_bf[...]) + bg_ref[0], SWIGLU_LIMIT)
            up = jnp.clip(_dot(xs, wu_bf[...]) + bu_ref[0], -SWIGLU_LIMIT, SWIGLU_LIMIT)
            glu = gate * jax.nn.sigmoid(gate * SWIGLU_ALPHA)
            acc[rows, :] += _dot(((up + 1) * glu).astype(BF16), wd_bf[...])

        def pair(c, carry):
            chunk(c * (2 * SUB), 2 * SUB)
            return carry

        lax.fori_loop(0, ns // 2, pair, 0)

        @pl.when(ns % 2 == 1)
        def _():
            chunk((ns - 1) * SUB, SUB)

    @pl.when(jnp.logical_and(j == N_FF_TILES - 1, ns > 0))
    def _():
        def out_copy(s):
            dst = pl.multiple_of(row0 + s * SUB, SUB)
            return pltpu.make_async_copy(acc.at[sub_rows(s), :],
                                         out_hbm.at[pl.ds(dst, SUB), :], osem)

        def start(s, carry):
            out_copy(s).start()
            return carry

        def wait(s, carry):
            out_copy(s).wait()
            return carry

        lax.fori_loop(0, ns, start, 0)
        lax.fori_loop(0, ns, wait, 0)

    @pl.when(jnp.logical_and(g == N_GROUPS - 1, j == N_FF_TILES - 1))
    def _():
        stage[0] = jnp.zeros((SUB, D_MODEL), F32)
        tail0 = tail_ref[0]
        n_tail = (N_ROWS - tail0) // SUB

        def tail_copy(s):
            dst = pl.multiple_of(tail0 + s * SUB, SUB)
            return pltpu.make_async_copy(stage.at[0], out_hbm.at[pl.ds(dst, SUB), :], osem)

        def start(s, carry):
            tail_copy(s).start()
            return carry

        def wait(s, carry):
            tail_copy(s).wait()
            return carry

        lax.fori_loop(0, n_tail, start, 0)
        lax.fori_loop(0, n_tail, wait, 0)


def _moe(layer, h2, plan, w_gu, b_gu, w_down, b_down):
    ge, gr, gn, tok, tail = plan
    nj = N_FF_TILES

    def jj(g, j, gn_ref):
        return jnp.where(gn_ref[g] > 0, j, nj - 1)

    sq = pl.Squeezed()
    grid_spec = pltpu.PrefetchScalarGridSpec(
        num_scalar_prefetch=5,
        grid=(N_GROUPS, nj),
        in_specs=[
            pl.BlockSpec(memory_space=pl.ANY),
            pl.BlockSpec((sq, 1, D_MODEL, FF_TILE),
                         lambda g, j, ge, gr, gn, *_: (layer, ge[g], 0, jj(g, j, gn))),
            pl.BlockSpec((sq, 1, D_MODEL, FF_TILE),
                         lambda g, j, ge, gr, gn, *_: (layer, ge[g], 0, nj + jj(g, j, gn))),
            pl.BlockSpec((sq, 1, FF_TILE, D_MODEL),
                         lambda g, j, ge, gr, gn, *_: (layer, ge[g], jj(g, j, gn), 0)),
            pl.BlockSpec((sq, 1, 1, FF_TILE),
                         lambda g, j, ge, gr, gn, *_: (layer, ge[g], 0, jj(g, j, gn))),
            pl.BlockSpec((sq, 1, 1, FF_TILE),
                         lambda g, j, ge, gr, gn, *_: (layer, ge[g], 0, nj + jj(g, j, gn))),
            pl.BlockSpec((sq, 1, 1, D_MODEL),
                         lambda g, j, ge, gr, gn, *_: (layer, ge[g], 0, 0)),
        ],
        out_specs=pl.BlockSpec(memory_space=pl.ANY),
        scratch_shapes=[
            pltpu.VMEM((GROUP_SUBS * SUB, D_MODEL), BF16),
            pltpu.VMEM((2, SUB, D_MODEL), F32),
            pltpu.VMEM((GROUP_SUBS * SUB, D_MODEL), F32),
            pltpu.VMEM((D_MODEL, FF_TILE), BF16),
            pltpu.VMEM((D_MODEL, FF_TILE), BF16),
            pltpu.VMEM((FF_TILE, D_MODEL), BF16),
            pltpu.SemaphoreType.DMA((2,)),
            pltpu.SemaphoreType.DMA(()),
        ],
    )
    return pl.pallas_call(
        _moe_kernel,
        grid_spec=grid_spec,
        out_shape=jax.ShapeDtypeStruct((N_ROWS, D_MODEL), F32),
        compiler_params=_cparams(2),
        name="moe",
    )(ge, gr, gn, tok, tail, h2, w_gu, w_gu, w_down, b_gu, b_gu, b_down)


def _plan(idx, rank, cnt):
    idx = idx[:, :TOP_K]
    rank = rank[:, :TOP_K]
    counts = cnt.reshape(N_EXPERTS).astype(I32)
    ntile = (counts + SUB - 1) // SUB
    start = (jnp.cumsum(ntile) - ntile) * SUB
    onehot = idx[..., None] == jnp.arange(N_EXPERTS, dtype=I32)
    pos = jnp.sum(jnp.where(onehot, start, 0), axis=-1) + rank
    pos = pos.reshape(N_SLOTS).astype(I32)
    tok = jnp.zeros((N_ROWS,), I32).at[pos].set(
        jnp.repeat(jnp.arange(N_TOK, dtype=I32), TOP_K), unique_indices=True)
    ngrp = (ntile + GROUP_SUBS - 1) // GROUP_SUBS
    gcum = jnp.cumsum(ngrp)
    total = gcum[-1]
    g = jnp.arange(N_GROUPS, dtype=I32)
    e_g = jnp.sum((gcum[None, :] <= g[:, None]).astype(I32), axis=1)
    e_g = jnp.minimum(e_g, N_EXPERTS - 1)
    local = g - (gcum - ngrp)[e_g]
    valid = g < total
    nsub = jnp.where(valid, jnp.minimum(GROUP_SUBS, ntile[e_g] - local * GROUP_SUBS), 0)
    row0 = jnp.where(valid, start[e_g] + local * GROUP_SUBS * SUB, 0)
    e_last = e_g[jnp.maximum(total - 1, 0)]
    e_g = jnp.where(valid, e_g, e_last)
    tail = (jnp.sum(ntile) * SUB).astype(I32).reshape(1)
    return pos, (e_g.astype(I32), row0.astype(I32), nsub.astype(I32), tok, tail)


COMB_TM = 256


def _combine_kernel(pos_ref, x1_ref, mod_ref, wgt_ref, y_hbm, fg_ref, o_ref, buf, sem, *, final):
    i = pl.program_id(0)
    n = pl.num_programs(0)
    tm = COMB_TM

    rows_per_iter = DMA_UNROLL // TOP_K

    def row_copy(p, slot, k, r):
        return pltpu.make_async_copy(y_hbm.at[pl.ds(p, 1), :],
                                     buf.at[slot, k, pl.ds(r, 1), :], sem.at[slot])

    def issue(tile, slot):
        base = tile * (tm * TOP_K)

        def body(c, carry):
            for u in range(rows_per_iter):
                r = c * rows_per_iter + u
                for k in range(TOP_K):
                    row_copy(pos_ref[base + r * TOP_K + k], slot, k, r).start()
            return carry

        lax.fori_loop(0, tm // rows_per_iter, body, 0)

    def wait(slot):
        def body(c, carry):
            for u in range(rows_per_iter):
                for k in range(TOP_K):
                    row_copy(0, slot, k, c * rows_per_iter + u).wait()
            return carry

        lax.fori_loop(0, tm // rows_per_iter, body, 0)

    @pl.when(i == 0)
    def _():
        issue(0, 0)

    @pl.when(i + 1 < n)
    def _():
        issue(i + 1, (i + 1) % 2)

    slot = i % 2
    wait(slot)
    w = wgt_ref[...]
    y = w[:, 0:1] * buf[slot, 0]
    for k in range(1, TOP_K):
        y = y + w[:, k:k + 1] * buf[slot, k]
    x2 = x1_ref[...] + mod_ref[0, 5:6, :] * y
    if final:
        o_ref[...] = _rms_scale(x2) * fg_ref[...]
    else:
        o_ref[...] = x2


def _combine(pos, x1, mod, wgt, y_sorted, final_g, final):
    tm = COMB_TM
    grid_spec = pltpu.PrefetchScalarGridSpec(
        num_scalar_prefetch=1,
        grid=(N_TOK // tm,),
        in_specs=[
            pl.BlockSpec((tm, D_MODEL), lambda i, pos: (i, 0)),
            pl.BlockSpec((1, N_ADA, D_MODEL), lambda i, pos: (i * tm // SEQ, 0, 0)),
            pl.BlockSpec((tm, LANES), lambda i, pos: (i, 0)),
            pl.BlockSpec(memory_space=pl.ANY),
            pl.BlockSpec((1, D_MODEL), lambda i, pos: (0, 0)),
        ],
        out_specs=pl.BlockSpec((tm, D_MODEL), lambda i, pos: (i, 0)),
        scratch_shapes=[
            pltpu.VMEM((2, TOP_K, tm, D_MODEL), F32),
            pltpu.SemaphoreType.DMA((2,)),
        ],
    )
    return pl.pallas_call(
        functools.partial(_combine_kernel, final=final),
        grid_spec=grid_spec,
        out_shape=jax.ShapeDtypeStruct((N_TOK, D_MODEL), F32),
        compiler_params=_cparams(1),
        name="combine_final" if final else "combine",
    )(pos, x1, mod, wgt, y_sorted, final_g.reshape(1, D_MODEL))


QKV_TM = 512
QKV_TN = 1024
QKV_NC = D_MODEL // QKV_TN


def _qkv_kernel(x_ref, mod_ref, ng_ref, kvg_ref, w_ref, c_ref, s1_ref, s2_ref, o_ref, h_scr):
    j = pl.program_id(1)
    which = j // QKV_NC

    @pl.when(j == 0)
    def _():
        xn = _rms_scale(x_ref[...])
        hq = (xn * ng_ref[0:1, :]) * (1 + mod_ref[0, 1:2, :]) + mod_ref[0, 0:1, :]
        h_scr[0] = hq.astype(BF16)
        h_scr[1] = (xn * kvg_ref[...]).astype(BF16)

    z = _dot(h_scr[jnp.minimum(which, 1)], w_ref[...])

    @pl.when(which < 2)
    def _():
        scale = jnp.where(which == 0, HEAD_DIM ** -0.5, 1.0).astype(F32)
        cos = c_ref[...]
        s1 = s1_ref[...]
        s2 = s2_ref[...]
        for hh in range(QKV_TN // HEAD_DIM):
            sl = slice(hh * HEAD_DIM, (hh + 1) * HEAD_DIM)
            zz = z[:, sl]
            rot = (zz * cos + pltpu.roll(zz, HEAD_DIM - ROT_HALF, 1) * s1
                   + pltpu.roll(zz, ROT_HALF, 1) * s2)
            o_ref[:, sl] = (rot * scale).astype(BF16)

    @pl.when(which == 2)
    def _():
        o_ref[...] = z.astype(BF16)


def _qkv(x2, mod, ng, kvg, w_qkv, cos_t, sin1_t, sin2_t):
    tm, tn = QKV_TM, QKV_TN
    row = lambda i, j: (i, 0)
    fixed = lambda i, j: (0, 0)
    return pl.pallas_call(
        _qkv_kernel,
        grid=(N_TOK // tm, 3 * QKV_NC),
        in_specs=[
            pl.BlockSpec((tm, D_MODEL), row),
            pl.BlockSpec((1, N_ADA, D_MODEL), lambda i, j: (i * tm // SEQ, 0, 0)),
            pl.BlockSpec((2, D_MODEL), fixed),
            pl.BlockSpec((1, D_MODEL), fixed),
            pl.BlockSpec((D_MODEL, tn), lambda i, j: (0, j)),
            pl.BlockSpec((tm, HEAD_DIM), row),
            pl.BlockSpec((tm, HEAD_DIM), row),
            pl.BlockSpec((tm, HEAD_DIM), row),
        ],
        out_specs=pl.BlockSpec((tm, tn), lambda i, j: (i, j)),
        out_shape=jax.ShapeDtypeStruct((N_TOK, 3 * D_MODEL), BF16),
        scratch_shapes=[pltpu.VMEM((2, tm, D_MODEL), BF16)],
        compiler_params=_cparams(2),
        name="qkv",
    )(x2, mod, ng, kvg.reshape(1, D_MODEL), w_qkv, cos_t, sin1_t, sin2_t)


def _attn_kernel(q_ref, k_ref, v_ref, o_ref, kaug_scr, km_scr):
    blk = MOBA_BLOCK
    kaug_scr[:, :HEAD_DIM] = k_ref[...]
    key_blk = lax.broadcasted_iota(I32, (SEQ, LANES), 0) >> 8
    key_lane = lax.broadcasted_iota(I32, (SEQ, LANES), 1)
    kaug_scr[:, HEAD_DIM:] = jnp.where(key_blk == key_lane, 1.0, 0.0).astype(BF16)
    km_scr[...] = jnp.zeros_like(km_scr)
    for jb in range(N_BLOCKS):
        kb = k_ref[jb * blk:(jb + 1) * blk, :].astype(F32)
        km_scr[jb:jb + 1, :] = jnp.mean(kb, axis=0, keepdims=True)
    km = km_scr[...].astype(BF16)
    lane = lax.broadcasted_iota(I32, (blk, LANES), 1)
    row = lax.broadcasted_iota(I32, (blk, blk), 0)
    col = lax.broadcasted_iota(I32, (blk, blk), 1)

    for qi in range(N_BLOCKS):
        q0 = qi * blk
        q = q_ref[q0:q0 + blk, :]
        keep = jnp.where(lane == qi, 1.0, 0.0)
        if qi > 0:
            gate = lax.dot_general(q, km, NT_DIMS, preferred_element_type=F32)
            past = lane < qi
            rem = jnp.where(past, gate, NEG_INF)
            for _ in range(MOBA_TOPK):
                m = jnp.max(rem, axis=-1, keepdims=True)
                first = jnp.min(jnp.where(rem == m, lane, LANES), axis=-1, keepdims=True)
                hit = lane == first
                keep = jnp.where(jnp.logical_and(hit, past), 1.0, keep)
                rem = jnp.where(hit, -jnp.inf, rem)
        bias = jnp.where(keep > 0.5, 0.0, NEG_INF).astype(BF16)
        q_aug = jnp.concatenate([q, bias], axis=1)
        s = lax.dot_general(q_aug, kaug_scr[0:q0 + blk, :], NT_DIMS, preferred_element_type=F32)
        s_own = jnp.where(col <= row, s[:, q0:], NEG_INF)
        m = jnp.max(s_own, axis=-1, keepdims=True)
        if qi > 0:
            s_past = s[:, :q0]
            m = jnp.maximum(m, jnp.max(s_past, axis=-1, keepdims=True))
        p_own = jnp.exp(s_own - m)
        l = jnp.sum(p_own, axis=-1, keepdims=True)
        acc = _dot(p_own.astype(BF16), v_ref[q0:q0 + blk, :])
        if qi > 0:
            p_past = jnp.exp(s_past - m)
            l = l + jnp.sum(p_past, axis=-1, keepdims=True)
            acc = acc + _dot(p_past.astype(BF16), v_ref[0:q0, :])
        o_ref[q0:q0 + blk, :] = (acc / l).astype(BF16)


def _attention(qkv):
    return pl.pallas_call(
        _attn_kernel,
        grid=(BATCH, N_HEADS),
        in_specs=[
            pl.BlockSpec((SEQ, HEAD_DIM), lambda b, h: (b, h)),
            pl.BlockSpec((SEQ, HEAD_DIM), lambda b, h: (b, N_HEADS + h)),
            pl.BlockSpec((SEQ, HEAD_DIM), lambda b, h: (b, 2 * N_HEADS + h)),
        ],
        out_specs=pl.BlockSpec((SEQ, HEAD_DIM), lambda b, h: (b, h)),
        out_shape=jax.ShapeDtypeStruct((N_TOK, D_MODEL), BF16),
        scratch_shapes=[pltpu.VMEM((SEQ, 2 * HEAD_DIM), BF16),
                        pltpu.VMEM((LANES, HEAD_DIM), F32)],
        compiler_params=_cparams(2),
        name="moba_attention",
    )(qkv, qkv, qkv)


OPROJ_TM = 512


def _oproj_kernel(a_ref, x_ref, mod_ref, ng_ref, w_ref, rw_ref, rb_ref,
                  x1_ref, h2_ref, idx_ref, wgt_ref, rank_ref, cnt_ref, cnt_scr):
    i = pl.program_id(0)

    @pl.when(i == 0)
    def _():
        cnt_scr[...] = jnp.zeros_like(cnt_scr)

    x1 = x_ref[...] + mod_ref[0, 2:3, :] * _dot(a_ref[...], w_ref[...])
    x1_ref[...] = x1
    _route(x1, i, ng_ref, mod_ref, rw_ref, rb_ref,
           h2_ref, idx_ref, wgt_ref, rank_ref, cnt_ref, cnt_scr)


def _oproj(attn, x2, mod, ng, w_o, rw, rb):
    tm = OPROJ_TM
    row = lambda i: (i, 0)
    fixed = lambda i: (0, 0)
    return pl.pallas_call(
        _oproj_kernel,
        grid=(N_TOK // tm,),
        in_specs=[
            pl.BlockSpec((tm, D_MODEL), row),
            pl.BlockSpec((tm, D_MODEL), row),
            pl.BlockSpec((1, N_ADA, D_MODEL), lambda i: (i * tm // SEQ, 0, 0)),
            pl.BlockSpec((2, D_MODEL), fixed),
            pl.BlockSpec((D_MODEL, D_MODEL), fixed, pipeline_mode=pl.Buffered(1)),
            pl.BlockSpec((D_MODEL, 2 * N_EXPERTS), fixed),
            pl.BlockSpec((1, N_EXPERTS), fixed),
        ],
        out_specs=(
            pl.BlockSpec((tm, D_MODEL), row),
            pl.BlockSpec((tm, D_MODEL), row),
            pl.BlockSpec((tm, LANES), row),
            pl.BlockSpec((tm, LANES), row),
            pl.BlockSpec((tm, LANES), row),
            pl.BlockSpec((1, N_EXPERTS), fixed),
        ),
        out_shape=_route_out_shapes(),
        scratch_shapes=[pltpu.VMEM((1, N_EXPERTS), F32)],
        compiler_params=_cparams(1),
        name="oproj",
    )(attn, x2, mod, ng, w_o, rw, rb)


def _split_hi_lo(w):
    hi = w.astype(BF16)
    lo = (w - hi.astype(F32)).astype(BF16)
    return jnp.concatenate([hi, lo], axis=1)


def _rope_tables(positions):
    inv = jnp.float32(ROPE_THETA) ** (-jnp.arange(0, ROT_DIM, 2, dtype=F32) / ROT_DIM)
    ang = positions.astype(F32).reshape(N_TOK, 1) * inv
    cos, sin = jnp.cos(ang), jnp.sin(ang)
    rest = HEAD_DIM - ROT_DIM
    cos_t = jnp.concatenate([cos, cos, jnp.ones((N_TOK, rest), F32)], axis=1)
    sin1_t = jnp.concatenate([-sin, jnp.zeros((N_TOK, HEAD_DIM - ROT_HALF), F32)], axis=1)
    sin2_t = jnp.concatenate([jnp.zeros((N_TOK, ROT_HALF), F32), sin,
                              jnp.zeros((N_TOK, rest), F32)], axis=1)
    return cos_t, sin1_t, sin2_t


def _moe_block(layer, x1, h2, idx, wgt, rank, cnt, mod, moe_w_gu, moe_b_gu, moe_w_down,
               moe_b_down, final_g, final):
    pos, plan = _plan(idx, rank, cnt)
    depth = moe_w_gu.shape[0]
    y_sorted = _moe(layer, h2, plan, moe_w_gu,
                    moe_b_gu.reshape(depth, N_EXPERTS, 1, 2 * D_FF), moe_w_down,
                    moe_b_down.reshape(depth, N_EXPERTS, 1, D_MODEL))
    return _combine(pos, x1, mod, wgt, y_sorted, final_g, final)


def kernel(x, c, positions, norm_g, ada_w, ada_b, conv_w_in, conv_w, conv_w_out, kv_norm_g, w_kv,
           attn_w_q, attn_w_o, router_w, router_b, moe_w_gu, moe_b_gu, moe_w_down, moe_b_down,
           final_g):
    xf = x.reshape(N_TOK, D_MODEL)
    ada = _ada(c, ada_w, ada_b)
    mod = ada[:, :BATCH].reshape(2, BATCH, N_ADA, D_MODEL)
    cos_t, sin1_t, sin2_t = _rope_tables(positions)

    x1, h2, idx, wgt, rank, cnt = _mixer0(
        xf, mod[0], norm_g[0], conv_w_in[0].astype(BF16), conv_w[0], conv_w_out[0].astype(BF16),
        _split_hi_lo(router_w[0]), router_b[0].reshape(1, N_EXPERTS))
    x2 = _moe_block(0, x1, h2, idx, wgt, rank, cnt, mod[0], moe_w_gu, moe_b_gu, moe_w_down,
                    moe_b_down, final_g, False)

    w_qkv = jnp.concatenate([attn_w_q[0], w_kv], axis=1).astype(BF16)
    qkv = _qkv(x2, mod[1], norm_g[1], kv_norm_g, w_qkv, cos_t, sin1_t, sin2_t)
    attn = _attention(qkv)
    x3, h2b, idx, wgt, rank, cnt = _oproj(
        attn, x2, mod[1], norm_g[1], attn_w_o[0].astype(BF16), _split_hi_lo(router_w[1]),
        router_b[1].reshape(1, N_EXPERTS))
    out = _moe_block(1, x3, h2b, idx, wgt, rank, cnt, mod[1], moe_w_gu, moe_b_gu, moe_w_down,
                     moe_b_down, final_g, True)
    return out.reshape(BATCH, SEQ, D_MODEL)
```

```python
import functools

import jax
import jax.numpy as jnp
from jax import lax
from jax.experimental import pallas as pl
from jax.experimental.pallas import tpu as pltpu

F32 = jnp.float32
BF16 = jnp.bfloat16
I32 = jnp.int32

D_MODEL = 2048
BATCH = 4
SEQ = 2048
N_TOK = BATCH * SEQ
N_HEADS = 16
HEAD_DIM = 128
ROT_DIM = 32
ROT_HALF = ROT_DIM // 2
ROPE_THETA = 500000.0
MOBA_BLOCK = 256
MOBA_TOPK = 3
N_BLOCKS = SEQ // MOBA_BLOCK
N_EXPERTS = 32
TOP_K = 4
D_FF = D_MODEL
SWIGLU_ALPHA = 1.702
SWIGLU_LIMIT = 7.0
NORM_EPS = 1e-5
NEG_INF = -1e30
N_ADA = 6

LANES = 128
SUBLANES = 8
VMEM_LIMIT = 56 * 1024 * 1024

SUB = 256
GROUP_SUBS = 8
N_SLOTS = N_TOK * TOP_K
N_ROWS = N_SLOTS + N_EXPERTS * SUB
N_GROUPS = N_EXPERTS + N_SLOTS // (SUB * GROUP_SUBS)
FF_TILE = 256
N_FF_TILES = D_FF // FF_TILE

NT_DIMS = (((1,), (1,)), ((), ()))


def _cparams(n_axes):
    return pltpu.CompilerParams(dimension_semantics=("arbitrary",) * n_axes,
                                vmem_limit_bytes=VMEM_LIMIT)


def _rms_scale(x):
    return x * lax.rsqrt(jnp.mean(x * x, axis=-1, keepdims=True) + NORM_EPS)


def _modulate(x, g, shift, scale):
    return (_rms_scale(x) * g) * (1 + scale) + shift


def _dot(a, b):
    return jnp.dot(a, b, preferred_element_type=F32)


ADA_TN = 1024


def _ada_kernel(c_ref, w_ref, b_ref, o_ref):
    c = c_ref[...]
    c_act = c * jax.nn.sigmoid(c)
    o_ref[0] = _dot(c_act.astype(BF16), w_ref[0].astype(BF16)) + b_ref[0]


def _ada(c, ada_w, ada_b):
    depth = ada_w.shape[0]
    n_out = ada_w.shape[2]
    c_pad = jnp.zeros((8, D_MODEL), F32).at[:BATCH].set(c)
    return pl.pallas_call(
        _ada_kernel,
        grid=(depth, n_out // ADA_TN),
        in_specs=[
            pl.BlockSpec((8, D_MODEL), lambda l, j: (0, 0)),
            pl.BlockSpec((1, D_MODEL, ADA_TN), lambda l, j: (l, 0, j)),
            pl.BlockSpec((1, 1, ADA_TN), lambda l, j: (l, 0, j)),
        ],
        out_specs=pl.BlockSpec((1, 8, ADA_TN), lambda l, j: (l, 0, j)),
        out_shape=jax.ShapeDtypeStruct((depth, 8, n_out), F32),
        compiler_params=_cparams(2),
        name="ada",
    )(c_pad, ada_w, ada_b.reshape(depth, 1, n_out))


def _route(x1, i, ng_ref, mod_ref, rw_ref, rb_ref,
           h2_ref, idx_ref, wgt_ref, rank_ref, cnt_ref, cnt_scr):
    tm = x1.shape[0]
    h2 = _modulate(x1, ng_ref[1:2, :], mod_ref[0, 3:4, :], mod_ref[0, 4:5, :])
    h2_ref[...] = h2
    h_hi = h2.astype(BF16)
    h_lo = (h2 - h_hi.astype(F32)).astype(BF16)
    r_hi = _dot(h_hi, rw_ref[...])
    r_lo = _dot(h_lo, rw_ref[...])
    logits = (r_hi[:, :N_EXPERTS] + r_hi[:, N_EXPERTS:] + r_lo[:, :N_EXPERTS]) + rb_ref[...]
    lane = lax.broadcasted_iota(I32, (tm, N_EXPERTS), 1)
    vals, idxs = [], []
    rem = logits
    for _ in range(TOP_K):
        m = jnp.max(rem, axis=-1, keepdims=True)
        ik = jnp.min(jnp.where(rem == m, lane, N_EXPERTS), axis=-1, keepdims=True)
        vals.append(m)
        idxs.append(ik)
        rem = jnp.where(lane == ik, -jnp.inf, rem)
    exps = [jnp.exp(v - vals[0]) for v in vals]
    den = exps[0] + exps[1] + exps[2] + exps[3]
    hot = jnp.zeros((tm, N_EXPERTS), F32)
    for ik in idxs:
        hot = jnp.where(lane == ik, 1.0, hot)
    row = lax.broadcasted_iota(I32, (tm, tm), 0)
    col = lax.broadcasted_iota(I32, (tm, tm), 1)
    tri = jnp.where(col < row, 1.0, 0.0).astype(BF16)
    prior = cnt_scr[...]
    rank_mat = _dot(tri, hot.astype(BF16)) + prior
    cnt_new = prior + jnp.sum(hot, axis=0, keepdims=True)
    cnt_scr[...] = cnt_new
    cnt_ref[...] = cnt_new
    lane_o = lax.broadcasted_iota(I32, (tm, LANES), 1)
    idx_o = jnp.zeros((tm, LANES), I32)
    wgt_o = jnp.zeros((tm, LANES), F32)
    rank_o = jnp.zeros((tm, LANES), I32)
    for k in range(TOP_K):
        rk = jnp.sum(jnp.where(lane == idxs[k], rank_mat, 0.0), axis=-1, keepdims=True)
        idx_o = jnp.where(lane_o == k, idxs[k], idx_o)
        wgt_o = jnp.where(lane_o == k, exps[k] / den, wgt_o)
        rank_o = jnp.where(lane_o == k, rk.astype(I32), rank_o)
    idx_ref[...] = idx_o
    wgt_ref[...] = wgt_o
    rank_ref[...] = rank_o


def _route_out_shapes():
    return (
        jax.ShapeDtypeStruct((N_TOK, D_MODEL), F32),
        jax.ShapeDtypeStruct((N_TOK, D_MODEL), F32),
        jax.ShapeDtypeStruct((N_TOK, LANES), I32),
        jax.ShapeDtypeStruct((N_TOK, LANES), F32),
        jax.ShapeDtypeStruct((N_TOK, LANES), I32),
        jax.ShapeDtypeStruct((1, N_EXPERTS), F32),
    )


MIX_TM = 512
MIX_TN = 512
MIX_NC = D_MODEL // MIX_TN


def _mixer0_kernel(x_ref, mod_ref, ng_ref, wb_ref, wc_ref, wu_ref, cw_ref, wo_ref, rw_ref, rb_ref,
                   x1_ref, h2_ref, idx_ref, wgt_ref, rank_ref, cnt_ref,
                   h_scr, vpad_scr, carry_scr, cnt_scr):
    i = pl.program_id(0)
    j = pl.program_id(1)
    tm = MIX_TM

    @pl.when(jnp.logical_and(i == 0, j == 0))
    def _():
        cnt_scr[...] = jnp.zeros_like(cnt_scr)
        carry_scr[...] = jnp.zeros_like(carry_scr)

    @pl.when(j == 0)
    def _():
        h = _modulate(x_ref[...], ng_ref[0:1, :], mod_ref[0, 0:1, :], mod_ref[0, 1:2, :])
        h_scr[...] = h.astype(BF16)
        x1_ref[...] = jnp.zeros_like(x1_ref)

    h = h_scr[...]
    b_gate = _dot(h, wb_ref[...])
    v = _dot(h, wc_ref[...]) * _dot(h, wu_ref[...])
    seq_start = (i % (SEQ // tm)) == 0
    vpad_scr[0:8, :] = jnp.where(seq_start, 0.0, carry_scr[j])
    vpad_scr[8:8 + tm, :] = v
    carry_scr[j] = v[tm - 8:tm, :]
    cw = cw_ref[...]
    conv = (cw[0:1, :] * vpad_scr[6:6 + tm, :] + cw[1:2, :] * vpad_scr[7:7 + tm, :]
            + cw[2:3, :] * v)
    x1_ref[...] += _dot((b_gate * conv).astype(BF16), wo_ref[...])

    @pl.when(j == MIX_NC - 1)
    def _():
        x1 = x_ref[...] + mod_ref[0, 2:3, :] * x1_ref[...]
        x1_ref[...] = x1
        _route(x1, i, ng_ref, mod_ref, rw_ref, rb_ref,
               h2_ref, idx_ref, wgt_ref, rank_ref, cnt_ref, cnt_scr)


def _mixer0(x, mod, ng, w_in, cw, w_out, rw, rb):
    tm, tn, nc = MIX_TM, MIX_TN, MIX_NC
    row = lambda i, j: (i, 0)
    fixed = lambda i, j: (0, 0)
    return pl.pallas_call(
        _mixer0_kernel,
        grid=(N_TOK // tm, nc),
        in_specs=[
            pl.BlockSpec((tm, D_MODEL), row),
            pl.BlockSpec((1, N_ADA, D_MODEL), lambda i, j: (i * tm // SEQ, 0, 0)),
            pl.BlockSpec((2, D_MODEL), fixed),
            pl.BlockSpec((D_MODEL, tn), lambda i, j: (0, j)),
            pl.BlockSpec((D_MODEL, tn), lambda i, j: (0, nc + j)),
            pl.BlockSpec((D_MODEL, tn), lambda i, j: (0, 2 * nc + j)),
            pl.BlockSpec((3, tn), lambda i, j: (0, j)),
            pl.BlockSpec((tn, D_MODEL), lambda i, j: (j, 0)),
            pl.BlockSpec((D_MODEL, 2 * N_EXPERTS), fixed),
            pl.BlockSpec((1, N_EXPERTS), fixed),
        ],
        out_specs=(
            pl.BlockSpec((tm, D_MODEL), row),
            pl.BlockSpec((tm, D_MODEL), row),
            pl.BlockSpec((tm, LANES), row),
            pl.BlockSpec((tm, LANES), row),
            pl.BlockSpec((tm, LANES), row),
            pl.BlockSpec((1, N_EXPERTS), fixed),
        ),
        out_shape=_route_out_shapes(),
        scratch_shapes=[
            pltpu.VMEM((tm, D_MODEL), BF16),
            pltpu.VMEM((tm + 8, tn), F32),
            pltpu.VMEM((nc, 8, tn), F32),
            pltpu.VMEM((1, N_EXPERTS), F32),
        ],
        compiler_params=_cparams(2),
        name="mixer0",
    )(x, mod, ng, w_in, w_in, w_in, cw, w_out, rw, rb)


def _moe_kernel(ge_ref, gr_ref, gn_ref, tok_ref, tail_ref,
                h_hbm, wg_ref, wu_ref, wd_ref, bg_ref, bu_ref, bd_ref,
                out_hbm,
                xg, stage, acc, gsem, osem):
    g = pl.program_id(0)
    j = pl.program_id(1)
    ns = gn_ref[g]
    row0 = gr_ref[g]

    def row_copy(tok, slot, c, u):
        return pltpu.make_async_copy(h_hbm.at[pl.ds(tok, 1), :],
                                     stage.at[slot, c, pl.ds(u, 1), :], gsem.at[slot])

    def issue_rows(s, slot):
        base = row0 + s * SUB

        def body(c, carry):
            for u in range(SUBLANES):
                row_copy(tok_ref[base + c * SUBLANES + u], slot, c, u).start()
            return carry

        lax.fori_loop(0, SUB // SUBLANES, body, 0)

    def wait_rows(slot):
        def body(c, carry):
            for u in range(SUBLANES):
                row_copy(0, slot, c, u).wait()
            return carry

        lax.fori_loop(0, SUB // SUBLANES, body, 0)

    def sub_rows(s):
        return pl.ds(pl.multiple_of(s * SUB, SUB), SUB)

    @pl.when(jnp.logical_and(j == 0, ns > 0))
    def _():
        issue_rows(0, 0)
        bias = jnp.broadcast_to(bd_ref[0], (SUB, D_MODEL))

        def body(s, carry):
            slot = s % 2

            @pl.when(s + 1 < ns)
            def _():
                issue_rows(s + 1, 1 - slot)

            acc[sub_rows(s), :] = bias
            wait_rows(slot)
            xg[sub_rows(s), :] = stage[slot].reshape(SUB, D_MODEL).astype(BF16)
            return carry

        lax.fori_loop(0, ns, body, 0)

    def out_copy(s):
        dst = pl.multiple_of(row0 + s * SUB, SUB)
        return pltpu.make_async_copy(acc.at[sub_rows(s), :],
                                     out_hbm.at[pl.ds(dst, SUB), :], osem)

    def chunk(s0, n_sub, write_out):
        rows = pl.ds(pl.multiple_of(s0 * SUB, SUB), n_sub * SUB)
        xs = xg[rows, :]
        gate = _dot(xs, wg_ref[0].astype(BF16)) + bg_ref[0]
        up = _dot(xs, wu_ref[0].astype(BF16)) + bu_ref[0]
        gate = jnp.minimum(gate, SWIGLU_LIMIT)
        up = jnp.clip(up, -SWIGLU_LIMIT, SWIGLU_LIMIT)
        glu = gate * jax.nn.sigmoid(gate * SWIGLU_ALPHA)
        acc[rows, :] += _dot(((up + 1) * glu).astype(BF16), wd_ref[0].astype(BF16))
        if write_out:
            for t in range(n_sub):
                out_copy(s0 + t).start()

    def run_chunks(write_out):
        n_pair = jnp.where(ns % 2 == 1, (ns - 3) // 2, ns // 2)

        def pair(c, carry):
            chunk(c * 2, 2, write_out)
            return carry

        lax.fori_loop(0, n_pair, pair, 0)

        @pl.when(jnp.logical_and(ns % 2 == 1, ns >= 3))
        def _():
            chunk(ns - 3, 3, write_out)

        @pl.when(ns == 1)
        def _():
            chunk(0, 1, write_out)

    @pl.when(jnp.logical_and(j < N_FF_TILES - 1, ns > 0))
    def _():
        run_chunks(False)

    @pl.when(jnp.logical_and(j == N_FF_TILES - 1, ns > 0))
    def _():
        run_chunks(True)

        def wait(s, carry):
            out_copy(s).wait()
            return carry

        lax.fori_loop(0, ns, wait, 0)

    @pl.when(jnp.logical_and(g == N_GROUPS - 1, j == N_FF_TILES - 1))
    def _():
        acc[0:SUB, :] = jnp.zeros((SUB, D_MODEL), F32)
        tail0 = tail_ref[0]
        n_tail = (N_ROWS - tail0) // SUB

        def tail_copy(s):
            dst = pl.multiple_of(tail0 + s * SUB, SUB)
            return pltpu.make_async_copy(acc.at[0:SUB, :], out_hbm.at[pl.ds(dst, SUB), :], osem)

        def start(s, carry):
            tail_copy(s).start()
            return carry

        def wait(s, carry):
            tail_copy(s).wait()
            return carry

        lax.fori_loop(0, n_tail, start, 0)
        lax.fori_loop(0, n_tail, wait, 0)


def _moe(layer, h2, plan, w_gu, b_gu, w_down, b_down):
    ge, gr, gn, tok, tail = plan
    nj = N_FF_TILES

    def jj(g, j, gn_ref):
        return jnp.where(gn_ref[g] > 0, j, nj - 1)

    sq = pl.Squeezed()
    grid_spec = pltpu.PrefetchScalarGridSpec(
        num_scalar_prefetch=5,
        grid=(N_GROUPS, nj),
        in_specs=[
            pl.BlockSpec(memory_space=pl.ANY),
            pl.BlockSpec((sq, 1, D_MODEL, FF_TILE),
                         lambda g, j, ge, gr, gn, *_: (layer, ge[g], 0, jj(g, j, gn))),
            pl.BlockSpec((sq, 1, D_MODEL, FF_TILE),
                         lambda g, j, ge, gr, gn, *_: (layer, ge[g], 0, nj + jj(g, j, gn))),
            pl.BlockSpec((sq, 1, FF_TILE, D_MODEL),
                         lambda g, j, ge, gr, gn, *_: (layer, ge[g], jj(g, j, gn), 0)),
            pl.BlockSpec((sq, 1, 1, FF_TILE),
                         lambda g, j, ge, gr, gn, *_: (layer, ge[g], 0, jj(g, j, gn))),
            pl.BlockSpec((sq, 1, 1, FF_TILE),
                         lambda g, j, ge, gr, gn, *_: (layer, ge[g], 0, nj + jj(g, j, gn))),
            pl.BlockSpec((sq, 1, 1, D_MODEL),
                         lambda g, j, ge, gr, gn, *_: (layer, ge[g], 0, 0)),
        ],
        out_specs=pl.BlockSpec(memory_space=pl.ANY),
        scratch_shapes=[
            pltpu.VMEM((GROUP_SUBS * SUB, D_MODEL), BF16),
            pltpu.VMEM((2, SUB // SUBLANES, SUBLANES, D_MODEL), F32),
            pltpu.VMEM((GROUP_SUBS * SUB, D_MODEL), F32),
            pltpu.SemaphoreType.DMA((2,)),
            pltpu.SemaphoreType.DMA(()),
        ],
    )
    return pl.pallas_call(
        _moe_kernel,
        grid_spec=grid_spec,
        out_shape=jax.ShapeDtypeStruct((N_ROWS, D_MODEL), F32),
        compiler_params=_cparams(2),
        name="moe",
    )(ge, gr, gn, tok, tail, h2, w_gu, w_gu, w_down, b_gu, b_gu, b_down)


def _plan(idx, rank, cnt):
    idx = idx[:, :TOP_K]
    rank = rank[:, :TOP_K]
    counts = cnt.reshape(N_EXPERTS).astype(I32)
    ntile = (counts + SUB - 1) // SUB
    start = (jnp.cumsum(ntile) - ntile) * SUB
    onehot = idx[..., None] == jnp.arange(N_EXPERTS, dtype=I32)
    pos = jnp.sum(jnp.where(onehot, start, 0), axis=-1) + rank
    pos = pos.reshape(N_SLOTS).astype(I32)
    tok = jnp.zeros((N_ROWS,), I32).at[pos].set(
        jnp.repeat(jnp.arange(N_TOK, dtype=I32), TOP_K), unique_indices=True)
    ngrp = (ntile + GROUP_SUBS - 1) // GROUP_SUBS
    gcum = jnp.cumsum(ngrp)
    total = gcum[-1]
    g = jnp.arange(N_GROUPS, dtype=I32)
    e_g = jnp.sum((gcum[None, :] <= g[:, None]).astype(I32), axis=1)
    e_g = jnp.minimum(e_g, N_EXPERTS - 1)
    local = g - (gcum - ngrp)[e_g]
    valid = g < total
    nsub = jnp.where(valid, jnp.minimum(GROUP_SUBS, ntile[e_g] - local * GROUP_SUBS), 0)
    row0 = jnp.where(valid, start[e_g] + local * GROUP_SUBS * SUB, 0)
    e_last = e_g[jnp.maximum(total - 1, 0)]
    e_g = jnp.where(valid, e_g, e_last)
    tail = (jnp.sum(ntile) * SUB).astype(I32).reshape(1)
    return pos, (e_g.astype(I32), row0.astype(I32), nsub.astype(I32), tok, tail)


COMB_TM = 256


def _combine_kernel(pos_ref, x1_ref, mod_ref, wgt_ref, y_hbm, fg_ref, o_ref, buf, sem, *, final):
    i = pl.program_id(0)
    n = pl.num_programs(0)
    tm = COMB_TM

    def row_copy(p, slot, k, c, u):
        return pltpu.make_async_copy(y_hbm.at[pl.ds(p, 1), :],
                                     buf.at[slot, k, c, pl.ds(u, 1), :], sem.at[slot])

    def issue(tile, slot):
        base = tile * (tm * TOP_K)

        def body(c, carry):
            for u in range(SUBLANES):
                for k in range(TOP_K):
                    p = pos_ref[base + (c * SUBLANES + u) * TOP_K + k]
                    row_copy(p, slot, k, c, u).start()
            return carry

        lax.fori_loop(0, tm // SUBLANES, body, 0)

    def wait(slot):
        def body(c, carry):
            for u in range(SUBLANES):
                for k in range(TOP_K):
                    row_copy(0, slot, k, c, u).wait()
            return carry

        lax.fori_loop(0, tm // SUBLANES, body, 0)

    @pl.when(i == 0)
    def _():
        issue(0, 0)

    @pl.when(i + 1 < n)
    def _():
        issue(i + 1, (i + 1) % 2)

    slot = i % 2
    wait(slot)
    w = wgt_ref[...]
    y = w[:, 0:1] * buf[slot, 0].reshape(tm, D_MODEL)
    for k in range(1, TOP_K):
        y = y + w[:, k:k + 1] * buf[slot, k].reshape(tm, D_MODEL)
    x2 = x1_ref[...] + mod_ref[0, 5:6, :] * y
    if final:
        o_ref[...] = _rms_scale(x2) * fg_ref[...]
    else:
        o_ref[...] = x2


def _combine(pos, x1, mod, wgt, y_sorted, final_g, final):
    tm = COMB_TM
    grid_spec = pltpu.PrefetchScalarGridSpec(
        num_scalar_prefetch=1,
        grid=(N_TOK // tm,),
        in_specs=[
            pl.BlockSpec((tm, D_MODEL), lambda i, pos: (i, 0)),
            pl.BlockSpec((1, N_ADA, D_MODEL), lambda i, pos: (i * tm // SEQ, 0, 0)),
            pl.BlockSpec((tm, LANES), lambda i, pos: (i, 0)),
            pl.BlockSpec(memory_space=pl.ANY),
            pl.BlockSpec((1, D_MODEL), lambda i, pos: (0, 0)),
        ],
        out_specs=pl.BlockSpec((tm, D_MODEL), lambda i, pos: (i, 0)),
        scratch_shapes=[
            pltpu.VMEM((2, TOP_K, tm // SUBLANES, SUBLANES, D_MODEL), F32),
            pltpu.SemaphoreType.DMA((2,)),
        ],
    )
    return pl.pallas_call(
        functools.partial(_combine_kernel, final=final),
        grid_spec=grid_spec,
        out_shape=jax.ShapeDtypeStruct((N_TOK, D_MODEL), F32),
        compiler_params=_cparams(1),
        name="combine_final" if final else "combine",
    )(pos, x1, mod, wgt, y_sorted, final_g.reshape(1, D_MODEL))


QKV_TM = 512
QKV_TN = 1024
QKV_NC = D_MODEL // QKV_TN


def _qkv_kernel(x_ref, mod_ref, ng_ref, kvg_ref, w_ref, c_ref, s1_ref, s2_ref, o_ref, h_scr):
    j = pl.program_id(1)
    which = j // QKV_NC

    @pl.when(j == 0)
    def _():
        xn = _rms_scale(x_ref[...])
        hq = (xn * ng_ref[0:1, :]) * (1 + mod_ref[0, 1:2, :]) + mod_ref[0, 0:1, :]
        h_scr[0] = hq.astype(BF16)
        h_scr[1] = (xn * kvg_ref[...]).astype(BF16)

    z = _dot(h_scr[jnp.minimum(which, 1)], w_ref[...])

    @pl.when(which < 2)
    def _():
        scale = jnp.where(which == 0, HEAD_DIM ** -0.5, 1.0).astype(F32)
        cos = c_ref[...]
        s1 = s1_ref[...]
        s2 = s2_ref[...]
        for hh in range(QKV_TN // HEAD_DIM):
            sl = slice(hh * HEAD_DIM, (hh + 1) * HEAD_DIM)
            zz = z[:, sl]
            rot = (zz * cos + pltpu.roll(zz, HEAD_DIM - ROT_HALF, 1) * s1
                   + pltpu.roll(zz, ROT_HALF, 1) * s2)
            o_ref[:, sl] = (rot * scale).astype(BF16)

    @pl.when(which == 2)
    def _():
        o_ref[...] = z.astype(BF16)


def _qkv(x2, mod, ng, kvg, w_qkv, cos_t, sin1_t, sin2_t):
    tm, tn = QKV_TM, QKV_TN
    row = lambda i, j: (i, 0)
    fixed = lambda i, j: (0, 0)
    return pl.pallas_call(
        _qkv_kernel,
        grid=(N_TOK // tm, 3 * QKV_NC),
        in_specs=[
            pl.BlockSpec((tm, D_MODEL), row),
            pl.BlockSpec((1, N_ADA, D_MODEL), lambda i, j: (i * tm // SEQ, 0, 0)),
            pl.BlockSpec((2, D_MODEL), fixed),
            pl.BlockSpec((1, D_MODEL), fixed),
            pl.BlockSpec((D_MODEL, tn), lambda i, j: (0, j)),
            pl.BlockSpec((tm, HEAD_DIM), row),
            pl.BlockSpec((tm, HEAD_DIM), row),
            pl.BlockSpec((tm, HEAD_DIM), row),
        ],
        out_specs=pl.BlockSpec((tm, tn), lambda i, j: (i, j)),
        out_shape=jax.ShapeDtypeStruct((N_TOK, 3 * D_MODEL), BF16),
        scratch_shapes=[pltpu.VMEM((2, tm, D_MODEL), BF16)],
        compiler_params=_cparams(2),
        name="qkv",
    )(x2, mod, ng, kvg.reshape(1, D_MODEL), w_qkv, cos_t, sin1_t, sin2_t)


def _attn_kernel(q_ref, k_ref, v_ref, o_ref, qaug_scr, kaug_scr, s_scr, p_scr, oacc_scr, km_scr):
    blk = MOBA_BLOCK
    kaug_scr[:, :HEAD_DIM] = k_ref[...]
    key_blk = lax.broadcasted_iota(I32, (SEQ, LANES), 0) >> 8
    key_lane = lax.broadcasted_iota(I32, (SEQ, LANES), 1)
    kaug_scr[:, HEAD_DIM:] = jnp.where(key_blk == key_lane, 1.0, 0.0).astype(BF16)
    km_scr[...] = jnp.zeros_like(km_scr)
    for jb in range(N_BLOCKS):
        kb = k_ref[jb * blk:(jb + 1) * blk, :].astype(F32)
        km_scr[jb:jb + 1, :] = jnp.mean(kb, axis=0, keepdims=True)
    km = km_scr[...].astype(BF16)

    q_all = q_ref[...]
    gate_t = lax.dot_general(km, q_all, NT_DIMS, preferred_element_type=F32)
    cand = lax.broadcasted_iota(I32, (SUBLANES, SEQ), 0)
    own = lax.broadcasted_iota(I32, (SUBLANES, SEQ), 1) >> 8
    past = cand < own
    rem = jnp.where(past, gate_t[0:N_BLOCKS, :], NEG_INF)
    keep = jnp.where(cand == own, 1.0, 0.0)
    for _ in range(MOBA_TOPK):
        m = jnp.max(rem, axis=0, keepdims=True)
        first = jnp.min(jnp.where(rem == m, cand, N_BLOCKS), axis=0, keepdims=True)
        hit = cand == first
        keep = jnp.where(jnp.logical_and(hit, past), 1.0, keep)
        rem = jnp.where(hit, -jnp.inf, rem)
    bias_t = jnp.concatenate([jnp.where(keep > 0.5, 0.0, NEG_INF),
                              jnp.zeros((LANES - SUBLANES, SEQ), F32)], axis=0)
    qaug_scr[:, :HEAD_DIM] = q_all
    qaug_scr[:, HEAD_DIM:] = bias_t.T.astype(BF16)

    for jb in range(N_BLOCKS):
        r0 = jb * blk
        s_scr[r0:, r0:r0 + blk] = lax.dot_general(
            qaug_scr[r0:, :], kaug_scr[r0:r0 + blk, :], NT_DIMS, preferred_element_type=F32)

    row = lax.broadcasted_iota(I32, (blk, blk), 0)
    col = lax.broadcasted_iota(I32, (blk, blk), 1)
    denom = []
    for qi in range(N_BLOCKS):
        q0 = qi * blk
        s_own = jnp.where(col <= row, s_scr[q0:q0 + blk, q0:q0 + blk], NEG_INF)
        m = jnp.max(s_own, axis=-1, keepdims=True)
        if qi > 0:
            s_past = s_scr[q0:q0 + blk, 0:q0]
            m = jnp.maximum(m, jnp.max(s_past, axis=-1, keepdims=True))
        p_own = jnp.exp(s_own - m)
        l = jnp.sum(p_own, axis=-1, keepdims=True)
        p_scr[q0:q0 + blk, q0:q0 + blk] = p_own.astype(BF16)
        if qi > 0:
            p_past = jnp.exp(s_past - m)
            l = l + jnp.sum(p_past, axis=-1, keepdims=True)
            p_scr[q0:q0 + blk, 0:q0] = p_past.astype(BF16)
        denom.append(l)

    for jb in range(N_BLOCKS):
        r0 = jb * blk
        pv = _dot(p_scr[r0:, r0:r0 + blk], v_ref[r0:r0 + blk, :])
        if jb == 0:
            oacc_scr[...] = pv
        else:
            oacc_scr[r0:, :] += pv
    for qi in range(N_BLOCKS):
        q0 = qi * blk
        o_ref[q0:q0 + blk, :] = (oacc_scr[q0:q0 + blk, :] / denom[qi]).astype(BF16)


def _attention(qkv):
    return pl.pallas_call(
        _attn_kernel,
        grid=(BATCH, N_HEADS),
        in_specs=[
            pl.BlockSpec((SEQ, HEAD_DIM), lambda b, h: (b, h)),
            pl.BlockSpec((SEQ, HEAD_DIM), lambda b, h: (b, N_HEADS + h)),
            pl.BlockSpec((SEQ, HEAD_DIM), lambda b, h: (b, 2 * N_HEADS + h)),
        ],
        out_specs=pl.BlockSpec((SEQ, HEAD_DIM), lambda b, h: (b, h)),
        out_shape=jax.ShapeDtypeStruct((N_TOK, D_MODEL), BF16),
        scratch_shapes=[pltpu.VMEM((SEQ, 2 * HEAD_DIM), BF16),
                        pltpu.VMEM((SEQ, 2 * HEAD_DIM), BF16),
                        pltpu.VMEM((SEQ, SEQ), F32),
                        pltpu.VMEM((SEQ, SEQ), BF16),
                        pltpu.VMEM((SEQ, HEAD_DIM), F32),
                        pltpu.VMEM((LANES, HEAD_DIM), F32)],
        compiler_params=_cparams(2),
        name="moba_attention",
    )(qkv, qkv, qkv)


OPROJ_TM = 512


def _oproj_kernel(a_ref, x_ref, mod_ref, ng_ref, w_ref, rw_ref, rb_ref,
                  x1_ref, h2_ref, idx_ref, wgt_ref, rank_ref, cnt_ref, cnt_scr):
    i = pl.program_id(0)

    @pl.when(i == 0)
    def _():
        cnt_scr[...] = jnp.zeros_like(cnt_scr)

    x1 = x_ref[...] + mod_ref[0, 2:3, :] * _dot(a_ref[...], w_ref[...])
    x1_ref[...] = x1
    _route(x1, i, ng_ref, mod_ref, rw_ref, rb_ref,
           h2_ref, idx_ref, wgt_ref, rank_ref, cnt_ref, cnt_scr)


def _oproj(attn, x2, mod, ng, w_o, rw, rb):
    tm = OPROJ_TM
    row = lambda i: (i, 0)
    fixed = lambda i: (0, 0)
    return pl.pallas_call(
        _oproj_kernel,
        grid=(N_TOK // tm,),
        in_specs=[
            pl.BlockSpec((tm, D_MODEL), row),
            pl.BlockSpec((tm, D_MODEL), row),
            pl.BlockSpec((1, N_ADA, D_MODEL), lambda i: (i * tm // SEQ, 0, 0)),
            pl.BlockSpec((2, D_MODEL), fixed),
            pl.BlockSpec((D_MODEL, D_MODEL), fixed, pipeline_mode=pl.Buffered(1)),
            pl.BlockSpec((D_MODEL, 2 * N_EXPERTS), fixed),
            pl.BlockSpec((1, N_EXPERTS), fixed),
        ],
        out_specs=(
            pl.BlockSpec((tm, D_MODEL), row),
            pl.BlockSpec((tm, D_MODEL), row),
            pl.BlockSpec((tm, LANES), row),
            pl.BlockSpec((tm, LANES), row),
            pl.BlockSpec((tm, LANES), row),
            pl.BlockSpec((1, N_EXPERTS), fixed),
        ),
        out_shape=_route_out_shapes(),
        scratch_shapes=[pltpu.VMEM((1, N_EXPERTS), F32)],
        compiler_params=_cparams(1),
        name="oproj",
    )(attn, x2, mod, ng, w_o, rw, rb)


def _split_hi_lo(w):
    hi = w.astype(BF16)
    lo = (w - hi.astype(F32)).astype(BF16)
    return jnp.concatenate([hi, lo], axis=1)


def _rope_tables(positions):
    inv = jnp.float32(ROPE_THETA) ** (-jnp.arange(0, ROT_DIM, 2, dtype=F32) / ROT_DIM)
    ang = positions.astype(F32).reshape(N_TOK, 1) * inv
    cos, sin = jnp.cos(ang), jnp.sin(ang)
    rest = HEAD_DIM - ROT_DIM
    cos_t = jnp.concatenate([cos, cos, jnp.ones((N_TOK, rest), F32)], axis=1)
    sin1_t = jnp.concatenate([-sin, jnp.zeros((N_TOK, HEAD_DIM - ROT_HALF), F32)], axis=1)
    sin2_t = jnp.concatenate([jnp.zeros((N_TOK, ROT_HALF), F32), sin,
                              jnp.zeros((N_TOK, rest), F32)], axis=1)
    return cos_t, sin1_t, sin2_t


def _moe_block(layer, x1, h2, idx, wgt, rank, cnt, mod, moe_w_gu, moe_b_gu, moe_w_down,
               moe_b_down, final_g, final):
    pos, plan = _plan(idx, rank, cnt)
    depth = moe_w_gu.shape[0]
    y_sorted = _moe(layer, h2, plan, moe_w_gu,
                    moe_b_gu.reshape(depth, N_EXPERTS, 1, 2 * D_FF), moe_w_down,
                    moe_b_down.reshape(depth, N_EXPERTS, 1, D_MODEL))
    return _combine(pos, x1, mod, wgt, y_sorted, final_g, final)


def kernel(x, c, positions, norm_g, ada_w, ada_b, conv_w_in, conv_w, conv_w_out, kv_norm_g, w_kv,
           attn_w_q, attn_w_o, router_w, router_b, moe_w_gu, moe_b_gu, moe_w_down, moe_b_down,
           final_g):
    xf = x.reshape(N_TOK, D_MODEL)
    ada = _ada(c, ada_w, ada_b)
    mod = ada[:, :BATCH].reshape(2, BATCH, N_ADA, D_MODEL)
    cos_t, sin1_t, sin2_t = _rope_tables(positions)

    x1, h2, idx, wgt, rank, cnt = _mixer0(
        xf, mod[0], norm_g[0], conv_w_in[0].astype(BF16), conv_w[0], conv_w_out[0].astype(BF16),
        _split_hi_lo(router_w[0]), router_b[0].reshape(1, N_EXPERTS))
    x2 = _moe_block(0, x1, h2, idx, wgt, rank, cnt, mod[0], moe_w_gu, moe_b_gu, moe_w_down,
                    moe_b_down, final_g, False)

    w_qkv = jnp.concatenate([attn_w_q[0], w_kv], axis=1).astype(BF16)
    qkv = _qkv(x2, mod[1], norm_g[1], kv_norm_g, w_qkv, cos_t, sin1_t, sin2_t)
    attn = _attention(qkv)
    x3, h2b, idx, wgt, rank, cnt = _oproj(
        attn, x2, mod[1], norm_g[1], attn_w_o[0].astype(BF16), _split_hi_lo(router_w[1]),
        router_b[1].reshape(1, N_EXPERTS))
    out = _moe_block(1, x3, h2b, idx, wgt, rank, cnt, mod[1], moe_w_gu, moe_b_gu, moe_w_down,
                     moe_b_down, final_g, True)
    return out.reshape(BATCH, SEQ, D_MODEL)
```

```python
import functools

import jax
import jax.numpy as jnp
from jax import lax
from jax.experimental import pallas as pl
from jax.experimental.pallas import tpu as pltpu

F32 = jnp.float32
BF16 = jnp.bfloat16
I32 = jnp.int32

D_MODEL = 2048
BATCH = 4
SEQ = 2048
N_TOK = BATCH * SEQ
N_HEADS = 16
HEAD_DIM = 128
ROT_DIM = 32
ROT_HALF = ROT_DIM // 2
ROPE_THETA = 500000.0
MOBA_BLOCK = 256
MOBA_TOPK = 3
N_BLOCKS = SEQ // MOBA_BLOCK
N_EXPERTS = 32
TOP_K = 4
D_FF = D_MODEL
SWIGLU_ALPHA = 1.702
SWIGLU_LIMIT = 7.0
NORM_EPS = 1e-5
NEG_INF = -1e30
N_ADA = 6

LANES = 128
SUBLANES = 8
VMEM_LIMIT = 56 * 1024 * 1024

SUB = 256
GROUP_SUBS = 6
PREFETCH_PER_SUB = 64
N_SLOTS = N_TOK * TOP_K
N_ROWS = N_SLOTS + N_EXPERTS * SUB
N_GROUPS = N_EXPERTS + N_SLOTS // (SUB * GROUP_SUBS)
FF_TILE = 256
N_FF_TILES = D_FF // FF_TILE

NT_DIMS = (((1,), (1,)), ((), ()))


def _cparams(n_axes):
    return pltpu.CompilerParams(dimension_semantics=("arbitrary",) * n_axes,
                                vmem_limit_bytes=VMEM_LIMIT)


def _rms_scale(x):
    return x * lax.rsqrt(jnp.mean(x * x, axis=-1, keepdims=True) + NORM_EPS)


def _modulate(x, g, shift, scale):
    return (_rms_scale(x) * g) * (1 + scale) + shift


def _dot(a, b):
    return jnp.dot(a, b, preferred_element_type=F32)


ADA_TN = 1024


def _ada_kernel(c_ref, w_ref, b_ref, o_ref):
    c = c_ref[...]
    c_act = c * jax.nn.sigmoid(c)
    o_ref[0] = _dot(c_act.astype(BF16), w_ref[0].astype(BF16)) + b_ref[0]


def _ada(c, ada_w, ada_b):
    depth = ada_w.shape[0]
    n_out = ada_w.shape[2]
    c_pad = jnp.zeros((8, D_MODEL), F32).at[:BATCH].set(c)
    return pl.pallas_call(
        _ada_kernel,
        grid=(depth, n_out // ADA_TN),
        in_specs=[
            pl.BlockSpec((8, D_MODEL), lambda l, j: (0, 0)),
            pl.BlockSpec((1, D_MODEL, ADA_TN), lambda l, j: (l, 0, j)),
            pl.BlockSpec((1, 1, ADA_TN), lambda l, j: (l, 0, j)),
        ],
        out_specs=pl.BlockSpec((1, 8, ADA_TN), lambda l, j: (l, 0, j)),
        out_shape=jax.ShapeDtypeStruct((depth, 8, n_out), F32),
        compiler_params=_cparams(2),
        name="ada",
    )(c_pad, ada_w, ada_b.reshape(depth, 1, n_out))


def _route(x1, i, ng_ref, mod_ref, rw_ref, rb_ref,
           h2_ref, idx_ref, wgt_ref, rank_ref, cnt_ref, cnt_scr):
    tm = x1.shape[0]
    h2 = _modulate(x1, ng_ref[1:2, :], mod_ref[0, 3:4, :], mod_ref[0, 4:5, :])
    h2_ref[...] = h2
    h_hi = h2.astype(BF16)
    h_lo = (h2 - h_hi.astype(F32)).astype(BF16)
    r_hi = _dot(h_hi, rw_ref[...])
    r_lo = _dot(h_lo, rw_ref[...])
    logits = (r_hi[:, :N_EXPERTS] + r_hi[:, N_EXPERTS:] + r_lo[:, :N_EXPERTS]) + rb_ref[...]
    lane = lax.broadcasted_iota(I32, (tm, N_EXPERTS), 1)
    vals, idxs = [], []
    rem = logits
    for _ in range(TOP_K):
        m = jnp.max(rem, axis=-1, keepdims=True)
        ik = jnp.min(jnp.where(rem == m, lane, N_EXPERTS), axis=-1, keepdims=True)
        vals.append(m)
        idxs.append(ik)
        rem = jnp.where(lane == ik, -jnp.inf, rem)
    exps = [jnp.exp(v - vals[0]) for v in vals]
    den = exps[0] + exps[1] + exps[2] + exps[3]
    hot = jnp.zeros((tm, N_EXPERTS), F32)
    for ik in idxs:
        hot = jnp.where(lane == ik, 1.0, hot)
    row = lax.broadcasted_iota(I32, (tm, tm), 0)
    col = lax.broadcasted_iota(I32, (tm, tm), 1)
    tri = jnp.where(col < row, 1.0, 0.0).astype(BF16)
    prior = cnt_scr[...]
    rank_mat = _dot(tri, hot.astype(BF16)) + prior
    cnt_new = prior + jnp.sum(hot, axis=0, keepdims=True)
    cnt_scr[...] = cnt_new
    cnt_ref[...] = cnt_new
    lane_o = lax.broadcasted_iota(I32, (tm, LANES), 1)
    idx_o = jnp.zeros((tm, LANES), I32)
    wgt_o = jnp.zeros((tm, LANES), F32)
    rank_o = jnp.zeros((tm, LANES), I32)
    for k in range(TOP_K):
        rk = jnp.sum(jnp.where(lane == idxs[k], rank_mat, 0.0), axis=-1, keepdims=True)
        idx_o = jnp.where(lane_o == k, idxs[k], idx_o)
        wgt_o = jnp.where(lane_o == k, exps[k] / den, wgt_o)
        rank_o = jnp.where(lane_o == k, rk.astype(I32), rank_o)
    idx_ref[...] = idx_o
    wgt_ref[...] = wgt_o
    rank_ref[...] = rank_o


def _route_out_shapes():
    return (
        jax.ShapeDtypeStruct((N_TOK, D_MODEL), F32),
        jax.ShapeDtypeStruct((N_TOK, D_MODEL), F32),
        jax.ShapeDtypeStruct((N_TOK, LANES), I32),
        jax.ShapeDtypeStruct((N_TOK, LANES), F32),
        jax.ShapeDtypeStruct((N_TOK, LANES), I32),
        jax.ShapeDtypeStruct((1, N_EXPERTS), F32),
    )


MIX_TM = 512
MIX_TN = 512
MIX_NC = D_MODEL // MIX_TN


def _mixer0_kernel(x_ref, mod_ref, ng_ref, wb_ref, wc_ref, wu_ref, cw_ref, wo_ref, rw_ref, rb_ref,
                   x1_ref, h2_ref, idx_ref, wgt_ref, rank_ref, cnt_ref,
                   h_scr, vpad_scr, carry_scr, cnt_scr):
    i = pl.program_id(0)
    j = pl.program_id(1)
    tm = MIX_TM

    @pl.when(jnp.logical_and(i == 0, j == 0))
    def _():
        cnt_scr[...] = jnp.zeros_like(cnt_scr)
        carry_scr[...] = jnp.zeros_like(carry_scr)

    @pl.when(j == 0)
    def _():
        h = _modulate(x_ref[...], ng_ref[0:1, :], mod_ref[0, 0:1, :], mod_ref[0, 1:2, :])
        h_scr[...] = h.astype(BF16)
        x1_ref[...] = jnp.zeros_like(x1_ref)

    h = h_scr[...]
    b_gate = _dot(h, wb_ref[...])
    v = _dot(h, wc_ref[...]) * _dot(h, wu_ref[...])
    seq_start = (i % (SEQ // tm)) == 0
    vpad_scr[0:8, :] = jnp.where(seq_start, 0.0, carry_scr[j])
    vpad_scr[8:8 + tm, :] = v
    carry_scr[j] = v[tm - 8:tm, :]
    cw = cw_ref[...]
    conv = (cw[0:1, :] * vpad_scr[6:6 + tm, :] + cw[1:2, :] * vpad_scr[7:7 + tm, :]
            + cw[2:3, :] * v)
    x1_ref[...] += _dot((b_gate * conv).astype(BF16), wo_ref[...])

    @pl.when(j == MIX_NC - 1)
    def _():
        x1 = x_ref[...] + mod_ref[0, 2:3, :] * x1_ref[...]
        x1_ref[...] = x1
        _route(x1, i, ng_ref, mod_ref, rw_ref, rb_ref,
               h2_ref, idx_ref, wgt_ref, rank_ref, cnt_ref, cnt_scr)


def _mixer0(x, mod, ng, w_in, cw, w_out, rw, rb):
    tm, tn, nc = MIX_TM, MIX_TN, MIX_NC
    row = lambda i, j: (i, 0)
    fixed = lambda i, j: (0, 0)
    return pl.pallas_call(
        _mixer0_kernel,
        grid=(N_TOK // tm, nc),
        in_specs=[
            pl.BlockSpec((tm, D_MODEL), row),
            pl.BlockSpec((1, N_ADA, D_MODEL), lambda i, j: (i * tm // SEQ, 0, 0)),
            pl.BlockSpec((2, D_MODEL), fixed),
            pl.BlockSpec((D_MODEL, tn), lambda i, j: (0, j)),
            pl.BlockSpec((D_MODEL, tn), lambda i, j: (0, nc + j)),
            pl.BlockSpec((D_MODEL, tn), lambda i, j: (0, 2 * nc + j)),
            pl.BlockSpec((3, tn), lambda i, j: (0, j)),
            pl.BlockSpec((tn, D_MODEL), lambda i, j: (j, 0)),
            pl.BlockSpec((D_MODEL, 2 * N_EXPERTS), fixed),
            pl.BlockSpec((1, N_EXPERTS), fixed),
        ],
        out_specs=(
            pl.BlockSpec((tm, D_MODEL), row),
            pl.BlockSpec((tm, D_MODEL), row),
            pl.BlockSpec((tm, LANES), row),
            pl.BlockSpec((tm, LANES), row),
            pl.BlockSpec((tm, LANES), row),
            pl.BlockSpec((1, N_EXPERTS), fixed),
        ),
        out_shape=_route_out_shapes(),
        scratch_shapes=[
            pltpu.VMEM((tm, D_MODEL), BF16),
            pltpu.VMEM((tm + 8, tn), F32),
            pltpu.VMEM((nc, 8, tn), F32),
            pltpu.VMEM((1, N_EXPERTS), F32),
        ],
        compiler_params=_cparams(2),
        name="mixer0",
    )(x, mod, ng, w_in, w_in, w_in, cw, w_out, rw, rb)


def _moe_kernel(ge_ref, gr_ref, gn_ref, tok_ref, tail_ref,
                h_hbm, wg_ref, wu_ref, wd_ref, bg_ref, bu_ref, bd_ref,
                out_hbm,
                xg, stage, acc, pf_ref, gsem, osem):
    g = pl.program_id(0)
    j = pl.program_id(1)
    ns = gn_ref[g]
    row0 = gr_ref[g]
    n_rows = ns * SUB
    g_next = jnp.minimum(g + 1, N_GROUPS - 1)
    next_row0 = gr_ref[g_next]
    next_rows = jnp.where(g + 1 < N_GROUPS, gn_ref[g_next], 0) * SUB

    def row_copy(tok, c, u):
        return pltpu.make_async_copy(h_hbm.at[pl.ds(tok, 1), :],
                                     stage.at[c, pl.ds(u, 1), :], gsem)

    def sub_rows(s):
        return pl.ds(pl.multiple_of(s * SUB, SUB), SUB)

    @pl.when(jnp.logical_and(g == 0, j == 0))
    def _():
        pf_ref[0] = 0

    @pl.when(jnp.logical_and(j == 0, ns > 0))
    def _():
        covered = jnp.minimum(pf_ref[0], n_rows)

        def fetch(c, carry):
            for u in range(SUBLANES):
                row_copy(tok_ref[row0 + c * SUBLANES + u], c, u).start()
            return carry

        lax.fori_loop(covered // SUBLANES, n_rows // SUBLANES, fetch, 0)

        def wait(c, carry):
            for u in range(SUBLANES):
                row_copy(0, c, u).wait()
            return carry

        lax.fori_loop(0, n_rows // SUBLANES, wait, 0)
        bias = jnp.broadcast_to(bd_ref[0], (SUB, D_MODEL))

        def cast(s, carry):
            tiles = pl.ds(pl.multiple_of(s * (SUB // SUBLANES), SUB // SUBLANES), SUB // SUBLANES)
            xg[sub_rows(s), :] = stage[tiles].reshape(SUB, D_MODEL).astype(BF16)
            acc[sub_rows(s), :] = bias
            return carry

        lax.fori_loop(0, ns, cast, 0)
        pf_ref[0] = 0

    def prefetch(n_offer):
        ptr = pf_ref[0]
        c0 = ptr // SUBLANES
        for u in range(n_offer):
            tok = tok_ref[next_row0 + ptr + u]
            row_copy(tok, c0 + u // SUBLANES, u % SUBLANES).start()
        pf_ref[0] = ptr + n_offer

    def out_copy(s):
        dst = pl.multiple_of(row0 + s * SUB, SUB)
        return pltpu.make_async_copy(acc.at[sub_rows(s), :],
                                     out_hbm.at[pl.ds(dst, SUB), :], osem)

    def chunk(s0, n_sub, write_out, with_prefetch):
        rows = pl.ds(pl.multiple_of(s0 * SUB, SUB), n_sub * SUB)
        xs = xg[rows, :]
        gate = _dot(xs, wg_ref[0].astype(BF16)) + bg_ref[0]
        up = _dot(xs, wu_ref[0].astype(BF16)) + bu_ref[0]
        gate = jnp.minimum(gate, SWIGLU_LIMIT)
        up = jnp.clip(up, -SWIGLU_LIMIT, SWIGLU_LIMIT)
        glu = gate * jax.nn.sigmoid(gate * SWIGLU_ALPHA)
        acc[rows, :] += _dot(((up + 1) * glu).astype(BF16), wd_ref[0].astype(BF16))
        if with_prefetch:
            prefetch(PREFETCH_PER_SUB * n_sub)
        if write_out:
            for t in range(n_sub):
                out_copy(s0 + t).start()

    def chunk_auto(s0, n_sub, write_out):
        batch_fits = pf_ref[0] + PREFETCH_PER_SUB * n_sub <= next_rows
        lax.cond(batch_fits,
                 lambda: chunk(s0, n_sub, write_out, True),
                 lambda: chunk(s0, n_sub, write_out, False))

    def run_chunks(write_out):
        def pair(c, carry):
            chunk_auto(c * 2, 2, write_out)
            return carry

        lax.fori_loop(0, ns // 2, pair, 0)

        @pl.when(ns % 2 == 1)
        def _():
            chunk_auto(ns - 1, 1, write_out)

    @pl.when(jnp.logical_and(j < N_FF_TILES - 1, ns > 0))
    def _():
        run_chunks(False)

    @pl.when(jnp.logical_and(j == N_FF_TILES - 1, ns > 0))
    def _():
        run_chunks(True)

        def wait(s, carry):
            out_copy(s).wait()
            return carry

        lax.fori_loop(0, ns, wait, 0)

    @pl.when(jnp.logical_and(g == N_GROUPS - 1, j == N_FF_TILES - 1))
    def _():
        acc[0:SUB, :] = jnp.zeros((SUB, D_MODEL), F32)
        tail0 = tail_ref[0]
        n_tail = (N_ROWS - tail0) // SUB

        def tail_copy(s):
            dst = pl.multiple_of(tail0 + s * SUB, SUB)
            return pltpu.make_async_copy(acc.at[0:SUB, :], out_hbm.at[pl.ds(dst, SUB), :], osem)

        def start(s, carry):
            tail_copy(s).start()
            return carry

        def wait(s, carry):
            tail_copy(s).wait()
            return carry

        lax.fori_loop(0, n_tail, start, 0)
        lax.fori_loop(0, n_tail, wait, 0)


def _moe(layer, h2, plan, w_gu, b_gu, w_down, b_down):
    ge, gr, gn, tok, tail = plan
    nj = N_FF_TILES

    def jj(g, j, gn_ref):
        return jnp.where(gn_ref[g] > 0, j, nj - 1)

    sq = pl.Squeezed()
    grid_spec = pltpu.PrefetchScalarGridSpec(
        num_scalar_prefetch=5,
        grid=(N_GROUPS, nj),
        in_specs=[
            pl.BlockSpec(memory_space=pl.ANY),
            pl.BlockSpec((sq, 1, D_MODEL, FF_TILE),
                         lambda g, j, ge, gr, gn, *_: (layer, ge[g], 0, jj(g, j, gn))),
            pl.BlockSpec((sq, 1, D_MODEL, FF_TILE),
                         lambda g, j, ge, gr, gn, *_: (layer, ge[g], 0, nj + jj(g, j, gn))),
            pl.BlockSpec((sq, 1, FF_TILE, D_MODEL),
                         lambda g, j, ge, gr, gn, *_: (layer, ge[g], jj(g, j, gn), 0)),
            pl.BlockSpec((sq, 1, 1, FF_TILE),
                         lambda g, j, ge, gr, gn, *_: (layer, ge[g], 0, jj(g, j, gn))),
            pl.BlockSpec((sq, 1, 1, FF_TILE),
                         lambda g, j, ge, gr, gn, *_: (layer, ge[g], 0, nj + jj(g, j, gn))),
            pl.BlockSpec((sq, 1, 1, D_MODEL),
                         lambda g, j, ge, gr, gn, *_: (layer, ge[g], 0, 0)),
        ],
        out_specs=pl.BlockSpec(memory_space=pl.ANY),
        scratch_shapes=[
            pltpu.VMEM((GROUP_SUBS * SUB, D_MODEL), BF16),
            pltpu.VMEM((GROUP_SUBS * SUB // SUBLANES, SUBLANES, D_MODEL), F32),
            pltpu.VMEM((GROUP_SUBS * SUB, D_MODEL), F32),
            pltpu.SMEM((1,), I32),
            pltpu.SemaphoreType.DMA(()),
            pltpu.SemaphoreType.DMA(()),
        ],
    )
    return pl.pallas_call(
        _moe_kernel,
        grid_spec=grid_spec,
        out_shape=jax.ShapeDtypeStruct((N_ROWS, D_MODEL), F32),
        compiler_params=_cparams(2),
        name="moe",
    )(ge, gr, gn, tok, tail, h2, w_gu, w_gu, w_down, b_gu, b_gu, b_down)


def _plan(idx, rank, cnt):
    idx = idx[:, :TOP_K]
    rank = rank[:, :TOP_K]
    counts = cnt.reshape(N_EXPERTS).astype(I32)
    ntile = (counts + SUB - 1) // SUB
    start = (jnp.cumsum(ntile) - ntile) * SUB
    onehot = idx[..., None] == jnp.arange(N_EXPERTS, dtype=I32)
    pos = jnp.sum(jnp.where(onehot, start, 0), axis=-1) + rank
    pos = pos.reshape(N_SLOTS).astype(I32)
    tok = jnp.zeros((N_ROWS,), I32).at[pos].set(
        jnp.repeat(jnp.arange(N_TOK, dtype=I32), TOP_K), unique_indices=True)
    ngrp = (ntile + GROUP_SUBS - 1) // GROUP_SUBS
    gcum = jnp.cumsum(ngrp)
    total = gcum[-1]
    g = jnp.arange(N_GROUPS, dtype=I32)
    e_g = jnp.sum((gcum[None, :] <= g[:, None]).astype(I32), axis=1)
    e_g = jnp.minimum(e_g, N_EXPERTS - 1)
    local = g - (gcum - ngrp)[e_g]
    valid = g < total
    nsub = jnp.where(valid, jnp.minimum(GROUP_SUBS, ntile[e_g] - local * GROUP_SUBS), 0)
    row0 = jnp.where(valid, start[e_g] + local * GROUP_SUBS * SUB, 0)
    e_last = e_g[jnp.maximum(total - 1, 0)]
    e_g = jnp.where(valid, e_g, e_last)
    tail = (jnp.sum(ntile) * SUB).astype(I32).reshape(1)
    return pos, (e_g.astype(I32), row0.astype(I32), nsub.astype(I32), tok, tail)


COMB_TM = 256


def _combine_kernel(pos_ref, x1_ref, mod_ref, wgt_ref, y_hbm, fg_ref, o_ref, buf, sem, *, final):
    i = pl.program_id(0)
    n = pl.num_programs(0)
    tm = COMB_TM

    def row_copy(p, slot, k, c, u):
        return pltpu.make_async_copy(y_hbm.at[pl.ds(p, 1), :],
                                     buf.at[slot, k, c, pl.ds(u, 1), :], sem.at[slot])

    def issue(tile, slot):
        base = tile * (tm * TOP_K)

        def body(c, carry):
            for u in range(SUBLANES):
                for k in range(TOP_K):
                    p = pos_ref[base + (c * SUBLANES + u) * TOP_K + k]
                    row_copy(p, slot, k, c, u).start()
            return carry

        lax.fori_loop(0, tm // SUBLANES, body, 0)

    def wait(slot):
        def body(c, carry):
            for u in range(SUBLANES):
                for k in range(TOP_K):
                    row_copy(0, slot, k, c, u).wait()
            return carry

        lax.fori_loop(0, tm // SUBLANES, body, 0)

    @pl.when(i == 0)
    def _():
        issue(0, 0)

    @pl.when(i + 1 < n)
    def _():
        issue(i + 1, (i + 1) % 2)

    slot = i % 2
    wait(slot)
    w = wgt_ref[...]
    y = w[:, 0:1] * buf[slot, 0].reshape(tm, D_MODEL)
    for k in range(1, TOP_K):
        y = y + w[:, k:k + 1] * buf[slot, k].reshape(tm, D_MODEL)
    x2 = x1_ref[...] + mod_ref[0, 5:6, :] * y
    if final:
        o_ref[...] = _rms_scale(x2) * fg_ref[...]
    else:
        o_ref[...] = x2


def _combine(pos, x1, mod, wgt, y_sorted, final_g, final):
    tm = COMB_TM
    grid_spec = pltpu.PrefetchScalarGridSpec(
        num_scalar_prefetch=1,
        grid=(N_TOK // tm,),
        in_specs=[
            pl.BlockSpec((tm, D_MODEL), lambda i, pos: (i, 0)),
            pl.BlockSpec((1, N_ADA, D_MODEL), lambda i, pos: (i * tm // SEQ, 0, 0)),
            pl.BlockSpec((tm, LANES), lambda i, pos: (i, 0)),
            pl.BlockSpec(memory_space=pl.ANY),
            pl.BlockSpec((1, D_MODEL), lambda i, pos: (0, 0)),
        ],
        out_specs=pl.BlockSpec((tm, D_MODEL), lambda i, pos: (i, 0)),
        scratch_shapes=[
            pltpu.VMEM((2, TOP_K, tm // SUBLANES, SUBLANES, D_MODEL), F32),
            pltpu.SemaphoreType.DMA((2,)),
        ],
    )
    return pl.pallas_call(
        functools.partial(_combine_kernel, final=final),
        grid_spec=grid_spec,
        out_shape=jax.ShapeDtypeStruct((N_TOK, D_MODEL), F32),
        compiler_params=_cparams(1),
        name="combine_final" if final else "combine",
    )(pos, x1, mod, wgt, y_sorted, final_g.reshape(1, D_MODEL))


QKV_TM = 512
QKV_TN = 1024
QKV_NC = D_MODEL // QKV_TN


def _qkv_kernel(x_ref, mod_ref, ng_ref, kvg_ref, w_ref, c_ref, s1_ref, s2_ref, o_ref, h_scr):
    j = pl.program_id(1)
    which = j // QKV_NC

    @pl.when(j == 0)
    def _():
        xn = _rms_scale(x_ref[...])
        hq = (xn * ng_ref[0:1, :]) * (1 + mod_ref[0, 1:2, :]) + mod_ref[0, 0:1, :]
        h_scr[0] = hq.astype(BF16)
        h_scr[1] = (xn * kvg_ref[...]).astype(BF16)

    z = _dot(h_scr[jnp.minimum(which, 1)], w_ref[...])

    @pl.when(which < 2)
    def _():
        scale = jnp.where(which == 0, HEAD_DIM ** -0.5, 1.0).astype(F32)
        cos = c_ref[...]
        s1 = s1_ref[...]
        s2 = s2_ref[...]
        for hh in range(QKV_TN // HEAD_DIM):
            sl = slice(hh * HEAD_DIM, (hh + 1) * HEAD_DIM)
            zz = z[:, sl]
            rot = (zz * cos + pltpu.roll(zz, HEAD_DIM - ROT_HALF, 1) * s1
                   + pltpu.roll(zz, ROT_HALF, 1) * s2)
            o_ref[:, sl] = (rot * scale).astype(BF16)

    @pl.when(which == 2)
    def _():
        o_ref[...] = z.astype(BF16)


def _qkv(x2, mod, ng, kvg, w_qkv, cos_t, sin1_t, sin2_t):
    tm, tn = QKV_TM, QKV_TN
    row = lambda i, j: (i, 0)
    fixed = lambda i, j: (0, 0)
    return pl.pallas_call(
        _qkv_kernel,
        grid=(N_TOK // tm, 3 * QKV_NC),
        in_specs=[
            pl.BlockSpec((tm, D_MODEL), row),
            pl.BlockSpec((1, N_ADA, D_MODEL), lambda i, j: (i * tm // SEQ, 0, 0)),
            pl.BlockSpec((2, D_MODEL), fixed),
            pl.BlockSpec((1, D_MODEL), fixed),
            pl.BlockSpec((D_MODEL, tn), lambda i, j: (0, j)),
            pl.BlockSpec((tm, HEAD_DIM), row),
            pl.BlockSpec((tm, HEAD_DIM), row),
            pl.BlockSpec((tm, HEAD_DIM), row),
        ],
        out_specs=pl.BlockSpec((tm, tn), lambda i, j: (i, j)),
        out_shape=jax.ShapeDtypeStruct((N_TOK, 3 * D_MODEL), BF16),
        scratch_shapes=[pltpu.VMEM((2, tm, D_MODEL), BF16)],
        compiler_params=_cparams(2),
        name="qkv",
    )(x2, mod, ng, kvg.reshape(1, D_MODEL), w_qkv, cos_t, sin1_t, sin2_t)


def _attn_kernel(q_ref, k_ref, v_ref, o_ref, qaug_scr, kaug_scr, s_scr, p_scr, oacc_scr, km_scr):
    blk = MOBA_BLOCK
    kaug_scr[:, :HEAD_DIM] = k_ref[...]
    key_blk = lax.broadcasted_iota(I32, (SEQ, LANES), 0) >> 8
    key_lane = lax.broadcasted_iota(I32, (SEQ, LANES), 1)
    kaug_scr[:, HEAD_DIM:] = jnp.where(key_blk == key_lane, 1.0, 0.0).astype(BF16)
    km_scr[...] = jnp.zeros_like(km_scr)
    for jb in range(N_BLOCKS):
        kb = k_ref[jb * blk:(jb + 1) * blk, :].astype(F32)
        km_scr[jb:jb + 1, :] = jnp.mean(kb, axis=0, keepdims=True)
    km = km_scr[...].astype(BF16)

    q_all = q_ref[...]
    gate_t = lax.dot_general(km, q_all, NT_DIMS, preferred_element_type=F32)
    cand = lax.broadcasted_iota(I32, (SUBLANES, SEQ), 0)
    own = lax.broadcasted_iota(I32, (SUBLANES, SEQ), 1) >> 8
    past = cand < own
    rem = jnp.where(past, gate_t[0:N_BLOCKS, :], NEG_INF)
    keep = jnp.where(cand == own, 1.0, 0.0)
    for _ in range(MOBA_TOPK):
        m = jnp.max(rem, axis=0, keepdims=True)
        first = jnp.min(jnp.where(rem == m, cand, N_BLOCKS), axis=0, keepdims=True)
        hit = cand == first
        keep = jnp.where(jnp.logical_and(hit, past), 1.0, keep)
        rem = jnp.where(hit, -jnp.inf, rem)
    bias_t = jnp.concatenate([jnp.where(keep > 0.5, 0.0, NEG_INF),
                              jnp.zeros((LANES - SUBLANES, SEQ), F32)], axis=0)
    qaug_scr[:, :HEAD_DIM] = q_all
    qaug_scr[:, HEAD_DIM:] = bias_t.T.astype(BF16)

    for jb in range(N_BLOCKS):
        r0 = jb * blk
        s_scr[r0:, r0:r0 + blk] = lax.dot_general(
            qaug_scr[r0:, :], kaug_scr[r0:r0 + blk, :], NT_DIMS, preferred_element_type=F32)

    row = lax.broadcasted_iota(I32, (blk, blk), 0)
    col = lax.broadcasted_iota(I32, (blk, blk), 1)
    denom = []
    for qi in range(N_BLOCKS):
        q0 = qi * blk
        s_own = jnp.where(col <= row, s_scr[q0:q0 + blk, q0:q0 + blk], NEG_INF)
        m = jnp.max(s_own, axis=-1, keepdims=True)
        if qi > 0:
            s_past = s_scr[q0:q0 + blk, 0:q0]
            m = jnp.maximum(m, jnp.max(s_past, axis=-1, keepdims=True))
        p_own = jnp.exp(s_own - m)
        l = jnp.sum(p_own, axis=-1, keepdims=True)
        p_scr[q0:q0 + blk, q0:q0 + blk] = p_own.astype(BF16)
        if qi > 0:
            p_past = jnp.exp(s_past - m)
            l = l + jnp.sum(p_past, axis=-1, keepdims=True)
            p_scr[q0:q0 + blk, 0:q0] = p_past.astype(BF16)
        denom.append(l)

    for jb in range(N_BLOCKS):
        r0 = jb * blk
        pv = _dot(p_scr[r0:, r0:r0 + blk], v_ref[r0:r0 + blk, :])
        if jb == 0:
            oacc_scr[...] = pv
        else:
            oacc_scr[r0:, :] += pv
    for qi in range(N_BLOCKS):
        q0 = qi * blk
        o_ref[q0:q0 + blk, :] = (oacc_scr[q0:q0 + blk, :] / denom[qi]).astype(BF16)


def _attention(qkv):
    return pl.pallas_call(
        _attn_kernel,
        grid=(BATCH, N_HEADS),
        in_specs=[
            pl.BlockSpec((SEQ, HEAD_DIM), lambda b, h: (b, h)),
            pl.BlockSpec((SEQ, HEAD_DIM), lambda b, h: (b, N_HEADS + h)),
            pl.BlockSpec((SEQ, HEAD_DIM), lambda b, h: (b, 2 * N_HEADS + h)),
        ],
        out_specs=pl.BlockSpec((SEQ, HEAD_DIM), lambda b, h: (b, h)),
        out_shape=jax.ShapeDtypeStruct((N_TOK, D_MODEL), BF16),
        scratch_shapes=[pltpu.VMEM((SEQ, 2 * HEAD_DIM), BF16),
                        pltpu.VMEM((SEQ, 2 * HEAD_DIM), BF16),
                        pltpu.VMEM((SEQ, SEQ), F32),
                        pltpu.VMEM((SEQ, SEQ), BF16),
                        pltpu.VMEM((SEQ, HEAD_DIM), F32),
                        pltpu.VMEM((LANES, HEAD_DIM), F32)],
        compiler_params=_cparams(2),
        name="moba_attention",
    )(qkv, qkv, qkv)


OPROJ_TM = 512


def _oproj_kernel(a_ref, x_ref, mod_ref, ng_ref, w_ref, rw_ref, rb_ref,
                  x1_ref, h2_ref, idx_ref, wgt_ref, rank_ref, cnt_ref, cnt_scr):
    i = pl.program_id(0)

    @pl.when(i == 0)
    def _():
        cnt_scr[...] = jnp.zeros_like(cnt_scr)

    x1 = x_ref[...] + mod_ref[0, 2:3, :] * _dot(a_ref[...], w_ref[...])
    x1_ref[...] = x1
    _route(x1, i, ng_ref, mod_ref, rw_ref, rb_ref,
           h2_ref, idx_ref, wgt_ref, rank_ref, cnt_ref, cnt_scr)


def _oproj(attn, x2, mod, ng, w_o, rw, rb):
    tm = OPROJ_TM
    row = lambda i: (i, 0)
    fixed = lambda i: (0, 0)
    return pl.pallas_call(
        _oproj_kernel,
        grid=(N_TOK // tm,),
        in_specs=[
            pl.BlockSpec((tm, D_MODEL), row),
            pl.BlockSpec((tm, D_MODEL), row),
            pl.BlockSpec((1, N_ADA, D_MODEL), lambda i: (i * tm // SEQ, 0, 0)),
            pl.BlockSpec((2, D_MODEL), fixed),
            pl.BlockSpec((D_MODEL, D_MODEL), fixed, pipeline_mode=pl.Buffered(1)),
            pl.BlockSpec((D_MODEL, 2 * N_EXPERTS), fixed),
            pl.BlockSpec((1, N_EXPERTS), fixed),
        ],
        out_specs=(
            pl.BlockSpec((tm, D_MODEL), row),
            pl.BlockSpec((tm, D_MODEL), row),
            pl.BlockSpec((tm, LANES), row),
            pl.BlockSpec((tm, LANES), row),
            pl.BlockSpec((tm, LANES), row),
            pl.BlockSpec((1, N_EXPERTS), fixed),
        ),
        out_shape=_route_out_shapes(),
        scratch_shapes=[pltpu.VMEM((1, N_EXPERTS), F32)],
        compiler_params=_cparams(1),
        name="oproj",
    )(attn, x2, mod, ng, w_o, rw, rb)


def _split_hi_lo(w):
    hi = w.astype(BF16)
    lo = (w - hi.astype(F32)).astype(BF16)
    return jnp.concatenate([hi, lo], axis=1)


def _rope_tables(positions):
    inv = jnp.float32(ROPE_THETA) ** (-jnp.arange(0, ROT_DIM, 2, dtype=F32) / ROT_DIM)
    ang = positions.astype(F32).reshape(N_TOK, 1) * inv
    cos, sin = jnp.cos(ang), jnp.sin(ang)
    rest = HEAD_DIM - ROT_DIM
    cos_t = jnp.concatenate([cos, cos, jnp.ones((N_TOK, rest), F32)], axis=1)
    sin1_t = jnp.concatenate([-sin, jnp.zeros((N_TOK, HEAD_DIM - ROT_HALF), F32)], axis=1)
    sin2_t = jnp.concatenate([jnp.zeros((N_TOK, ROT_HALF), F32), sin,
                              jnp.zeros((N_TOK, rest), F32)], axis=1)
    return cos_t, sin1_t, sin2_t


def _moe_block(layer, x1, h2, idx, wgt, rank, cnt, mod, moe_w_gu, moe_b_gu, moe_w_down,
               moe_b_down, final_g, final):
    pos, plan = _plan(idx, rank, cnt)
    depth = moe_w_gu.shape[0]
    y_sorted = _moe(layer, h2, plan, moe_w_gu,
                    moe_b_gu.reshape(depth, N_EXPERTS, 1, 2 * D_FF), moe_w_down,
                    moe_b_down.reshape(depth, N_EXPERTS, 1, D_MODEL))
    return _combine(pos, x1, mod, wgt, y_sorted, final_g, final)


def kernel(x, c, positions, norm_g, ada_w, ada_b, conv_w_in, conv_w, conv_w_out, kv_norm_g, w_kv,
           attn_w_q, attn_w_o, router_w, router_b, moe_w_gu, moe_b_gu, moe_w_down, moe_b_down,
           final_g):
    xf = x.reshape(N_TOK, D_MODEL)
    ada = _ada(c, ada_w, ada_b)
    mod = ada[:, :BATCH].reshape(2, BATCH, N_ADA, D_MODEL)
    cos_t, sin1_t, sin2_t = _rope_tables(positions)

    x1, h2, idx, wgt, rank, cnt = _mixer0(
        xf, mod[0], norm_g[0], conv_w_in[0].astype(BF16), conv_w[0], conv_w_out[0].astype(BF16),
        _split_hi_lo(router_w[0]), router_b[0].reshape(1, N_EXPERTS))
    x2 = _moe_block(0, x1, h2, idx, wgt, rank, cnt, mod[0], moe_w_gu, moe_b_gu, moe_w_down,
                    moe_b_down, final_g, False)

    w_qkv = jnp.concatenate([attn_w_q[0], w_kv], axis=1).astype(BF16)
    qkv = _qkv(x2, mod[1], norm_g[1], kv_norm_g, w_qkv, cos_t, sin1_t, sin2_t)
    attn = _attention(qkv)
    x3, h2b, idx, wgt, rank, cnt = _oproj(
        attn, x2, mod[1], norm_g[1], attn_w_o[0].astype(BF16), _split_hi_lo(router_w[1]),
        router_b[1].reshape(1, N_EXPERTS))
    out = _moe_block(1, x3, h2b, idx, wgt, rank, cnt, mod[1], moe_w_gu, moe_b_gu, moe_w_down,
                     moe_b_down, final_g, True)
    return out.reshape(BATCH, SEQ, D_MODEL)
```

```python
import functools

import jax
import jax.numpy as jnp
from jax import lax
from jax.experimental import pallas as pl
from jax.experimental.pallas import tpu as pltpu

F32 = jnp.float32
BF16 = jnp.bfloat16
I32 = jnp.int32

D_MODEL = 2048
BATCH = 4
SEQ = 2048
N_TOK = BATCH * SEQ
N_HEADS = 16
HEAD_DIM = 128
ROT_DIM = 32
ROT_HALF = ROT_DIM // 2
ROPE_THETA = 500000.0
MOBA_BLOCK = 256
MOBA_TOPK = 3
N_BLOCKS = SEQ // MOBA_BLOCK
N_EXPERTS = 32
TOP_K = 4
D_FF = D_MODEL
SWIGLU_ALPHA = 1.702
SWIGLU_LIMIT = 7.0
NORM_EPS = 1e-5
NEG_INF = -1e30
N_ADA = 6

LANES = 128
SUBLANES = 8
VMEM_LIMIT = 56 * 1024 * 1024
MOE_VMEM_LIMIT = 60 * 1024 * 1024

SUB = 256
GROUP_SUBS = 8
MAIN_SUBS = 4
PREFETCH_PER_SUB = 64
N_SLOTS = N_TOK * TOP_K
N_ROWS = N_SLOTS + N_EXPERTS * SUB
N_GROUPS = N_EXPERTS + N_SLOTS // (SUB * GROUP_SUBS)
FF_TILE = 256
N_FF_TILES = D_FF // FF_TILE

NT_DIMS = (((1,), (1,)), ((), ()))


def _cparams(n_axes, vmem_limit=VMEM_LIMIT):
    return pltpu.CompilerParams(dimension_semantics=("arbitrary",) * n_axes,
                                vmem_limit_bytes=vmem_limit)


def _rms_scale(x):
    return x * lax.rsqrt(jnp.mean(x * x, axis=-1, keepdims=True) + NORM_EPS)


def _modulate(x, g, shift, scale):
    return (_rms_scale(x) * g) * (1 + scale) + shift


def _dot(a, b):
    return jnp.dot(a, b, preferred_element_type=F32)


ADA_TN = 1024


def _ada_kernel(c_ref, w_ref, b_ref, o_ref):
    c = c_ref[...]
    c_act = c * jax.nn.sigmoid(c)
    o_ref[0] = _dot(c_act.astype(BF16), w_ref[0].astype(BF16)) + b_ref[0]


def _ada(c, ada_w, ada_b):
    depth = ada_w.shape[0]
    n_out = ada_w.shape[2]
    c_pad = jnp.zeros((8, D_MODEL), F32).at[:BATCH].set(c)
    return pl.pallas_call(
        _ada_kernel,
        grid=(depth, n_out // ADA_TN),
        in_specs=[
            pl.BlockSpec((8, D_MODEL), lambda l, j: (0, 0)),
            pl.BlockSpec((1, D_MODEL, ADA_TN), lambda l, j: (l, 0, j)),
            pl.BlockSpec((1, 1, ADA_TN), lambda l, j: (l, 0, j)),
        ],
        out_specs=pl.BlockSpec((1, 8, ADA_TN), lambda l, j: (l, 0, j)),
        out_shape=jax.ShapeDtypeStruct((depth, 8, n_out), F32),
        compiler_params=_cparams(2),
        name="ada",
    )(c_pad, ada_w, ada_b.reshape(depth, 1, n_out))


def _route(x1, i, ng_ref, mod_ref, rw_ref, rb_ref,
           h2_ref, idx_ref, wgt_ref, rank_ref, cnt_ref, cnt_scr):
    tm = x1.shape[0]
    h2 = _modulate(x1, ng_ref[1:2, :], mod_ref[0, 3:4, :], mod_ref[0, 4:5, :])
    h2_ref[...] = h2
    h_hi = h2.astype(BF16)
    h_lo = (h2 - h_hi.astype(F32)).astype(BF16)
    r_hi = _dot(h_hi, rw_ref[...])
    r_lo = _dot(h_lo, rw_ref[...])
    logits = (r_hi[:, :N_EXPERTS] + r_hi[:, N_EXPERTS:] + r_lo[:, :N_EXPERTS]) + rb_ref[...]
    lane = lax.broadcasted_iota(I32, (tm, N_EXPERTS), 1)
    vals, idxs = [], []
    rem = logits
    for _ in range(TOP_K):
        m = jnp.max(rem, axis=-1, keepdims=True)
        ik = jnp.min(jnp.where(rem == m, lane, N_EXPERTS), axis=-1, keepdims=True)
        vals.append(m)
        idxs.append(ik)
        rem = jnp.where(lane == ik, -jnp.inf, rem)
    exps = [jnp.exp(v - vals[0]) for v in vals]
    den = exps[0] + exps[1] + exps[2] + exps[3]
    hot = jnp.zeros((tm, N_EXPERTS), F32)
    for ik in idxs:
        hot = jnp.where(lane == ik, 1.0, hot)
    row = lax.broadcasted_iota(I32, (tm, tm), 0)
    col = lax.broadcasted_iota(I32, (tm, tm), 1)
    tri = jnp.where(col < row, 1.0, 0.0).astype(BF16)
    prior = cnt_scr[...]
    rank_mat = _dot(tri, hot.astype(BF16)) + prior
    cnt_new = prior + jnp.sum(hot, axis=0, keepdims=True)
    cnt_scr[...] = cnt_new
    cnt_ref[...] = cnt_new
    lane_o = lax.broadcasted_iota(I32, (tm, LANES), 1)
    idx_o = jnp.zeros((tm, LANES), I32)
    wgt_o = jnp.zeros((tm, LANES), F32)
    rank_o = jnp.zeros((tm, LANES), I32)
    for k in range(TOP_K):
        rk = jnp.sum(jnp.where(lane == idxs[k], rank_mat, 0.0), axis=-1, keepdims=True)
        idx_o = jnp.where(lane_o == k, idxs[k], idx_o)
        wgt_o = jnp.where(lane_o == k, exps[k] / den, wgt_o)
        rank_o = jnp.where(lane_o == k, rk.astype(I32), rank_o)
    idx_ref[...] = idx_o
    wgt_ref[...] = wgt_o
    rank_ref[...] = rank_o


def _route_out_shapes():
    return (
        jax.ShapeDtypeStruct((N_TOK, D_MODEL), F32),
        jax.ShapeDtypeStruct((N_TOK, D_MODEL), F32),
        jax.ShapeDtypeStruct((N_TOK, LANES), I32),
        jax.ShapeDtypeStruct((N_TOK, LANES), F32),
        jax.ShapeDtypeStruct((N_TOK, LANES), I32),
        jax.ShapeDtypeStruct((1, N_EXPERTS), F32),
    )


MIX_TM = 512
MIX_TN = 512
MIX_NC = D_MODEL // MIX_TN


def _mixer0_kernel(x_ref, mod_ref, ng_ref, wb_ref, wc_ref, wu_ref, cw_ref, wo_ref, rw_ref, rb_ref,
                   x1_ref, h2_ref, idx_ref, wgt_ref, rank_ref, cnt_ref,
                   h_scr, vpad_scr, carry_scr, cnt_scr):
    i = pl.program_id(0)
    j = pl.program_id(1)
    tm = MIX_TM

    @pl.when(jnp.logical_and(i == 0, j == 0))
    def _():
        cnt_scr[...] = jnp.zeros_like(cnt_scr)
        carry_scr[...] = jnp.zeros_like(carry_scr)

    @pl.when(j == 0)
    def _():
        h = _modulate(x_ref[...], ng_ref[0:1, :], mod_ref[0, 0:1, :], mod_ref[0, 1:2, :])
        h_scr[...] = h.astype(BF16)
        x1_ref[...] = jnp.zeros_like(x1_ref)

    h = h_scr[...]
    b_gate = _dot(h, wb_ref[...])
    v = _dot(h, wc_ref[...]) * _dot(h, wu_ref[...])
    seq_start = (i % (SEQ // tm)) == 0
    vpad_scr[0:8, :] = jnp.where(seq_start, 0.0, carry_scr[j])
    vpad_scr[8:8 + tm, :] = v
    carry_scr[j] = v[tm - 8:tm, :]
    cw = cw_ref[...]
    conv = (cw[0:1, :] * vpad_scr[6:6 + tm, :] + cw[1:2, :] * vpad_scr[7:7 + tm, :]
            + cw[2:3, :] * v)
    x1_ref[...] += _dot((b_gate * conv).astype(BF16), wo_ref[...])

    @pl.when(j == MIX_NC - 1)
    def _():
        x1 = x_ref[...] + mod_ref[0, 2:3, :] * x1_ref[...]
        x1_ref[...] = x1
        _route(x1, i, ng_ref, mod_ref, rw_ref, rb_ref,
               h2_ref, idx_ref, wgt_ref, rank_ref, cnt_ref, cnt_scr)


def _mixer0(x, mod, ng, w_in, cw, w_out, rw, rb):
    tm, tn, nc = MIX_TM, MIX_TN, MIX_NC
    row = lambda i, j: (i, 0)
    fixed = lambda i, j: (0, 0)
    return pl.pallas_call(
        _mixer0_kernel,
        grid=(N_TOK // tm, nc),
        in_specs=[
            pl.BlockSpec((tm, D_MODEL), row),
            pl.BlockSpec((1, N_ADA, D_MODEL), lambda i, j: (i * tm // SEQ, 0, 0)),
            pl.BlockSpec((2, D_MODEL), fixed),
            pl.BlockSpec((D_MODEL, tn), lambda i, j: (0, j)),
            pl.BlockSpec((D_MODEL, tn), lambda i, j: (0, nc + j)),
            pl.BlockSpec((D_MODEL, tn), lambda i, j: (0, 2 * nc + j)),
            pl.BlockSpec((3, tn), lambda i, j: (0, j)),
            pl.BlockSpec((tn, D_MODEL), lambda i, j: (j, 0)),
            pl.BlockSpec((D_MODEL, 2 * N_EXPERTS), fixed),
            pl.BlockSpec((1, N_EXPERTS), fixed),
        ],
        out_specs=(
            pl.BlockSpec((tm, D_MODEL), row),
            pl.BlockSpec((tm, D_MODEL), row),
            pl.BlockSpec((tm, LANES), row),
            pl.BlockSpec((tm, LANES), row),
            pl.BlockSpec((tm, LANES), row),
            pl.BlockSpec((1, N_EXPERTS), fixed),
        ),
        out_shape=_route_out_shapes(),
        scratch_shapes=[
            pltpu.VMEM((tm, D_MODEL), BF16),
            pltpu.VMEM((tm + 8, tn), F32),
            pltpu.VMEM((nc, 8, tn), F32),
            pltpu.VMEM((1, N_EXPERTS), F32),
        ],
        compiler_params=_cparams(2),
        name="mixer0",
    )(x, mod, ng, w_in, w_in, w_in, cw, w_out, rw, rb)


def _moe_kernel(ge_ref, gr_ref, gn_ref, tok_ref, tail_ref,
                h_hbm, wg_ref, wu_ref, wd_ref, bg_ref, bu_ref, bd_ref,
                out_hbm,
                xg, stage, acc, pf_ref, gsem, osem):
    g = pl.program_id(0)
    j = pl.program_id(1)
    ns = gn_ref[g]
    row0 = gr_ref[g]
    n_rows = ns * SUB
    g_next = jnp.minimum(g + 1, N_GROUPS - 1)
    next_row0 = gr_ref[g_next]
    next_rows = jnp.where(g + 1 < N_GROUPS, gn_ref[g_next], 0) * SUB

    def row_copy(tok, c, u):
        return pltpu.make_async_copy(h_hbm.at[pl.ds(tok, 1), :],
                                     stage.at[c, pl.ds(u, 1), :], gsem)

    def sub_rows(s):
        return pl.ds(pl.multiple_of(s * SUB, SUB), SUB)

    @pl.when(jnp.logical_and(g == 0, j == 0))
    def _():
        pf_ref[0] = 0

    @pl.when(jnp.logical_and(j == 0, ns > 0))
    def _():
        covered = jnp.minimum(pf_ref[0], n_rows)

        def fetch(c, carry):
            for u in range(SUBLANES):
                row_copy(tok_ref[row0 + c * SUBLANES + u], c, u).start()
            return carry

        lax.fori_loop(covered // SUBLANES, n_rows // SUBLANES, fetch, 0)

        def wait(c, carry):
            for u in range(SUBLANES):
                row_copy(0, c, u).wait()
            return carry

        lax.fori_loop(0, n_rows // SUBLANES, wait, 0)
        bias = jnp.broadcast_to(bd_ref[0], (SUB, D_MODEL))

        def cast(s, carry):
            tiles = pl.ds(pl.multiple_of(s * (SUB // SUBLANES), SUB // SUBLANES), SUB // SUBLANES)
            xg[sub_rows(s), :] = stage[tiles].reshape(SUB, D_MODEL).astype(BF16)
            acc[sub_rows(s), :] = bias
            return carry

        lax.fori_loop(0, ns, cast, 0)
        pf_ref[0] = 0

    def prefetch(n_offer):
        ptr = pf_ref[0]
        c0 = ptr // SUBLANES
        for u in range(n_offer):
            tok = tok_ref[next_row0 + ptr + u]
            row_copy(tok, c0 + u // SUBLANES, u % SUBLANES).start()
        pf_ref[0] = ptr + n_offer

    def out_copy(s):
        dst = pl.multiple_of(row0 + s * SUB, SUB)
        return pltpu.make_async_copy(acc.at[sub_rows(s), :],
                                     out_hbm.at[pl.ds(dst, SUB), :], osem)

    last_tile = j == N_FF_TILES - 1

    def chunk(s0, n_sub, with_prefetch):
        rows = pl.ds(pl.multiple_of(s0 * SUB, SUB), n_sub * SUB)
        xs = xg[rows, :]
        gate = _dot(xs, wg_ref[0].astype(BF16)) + bg_ref[0]
        up = _dot(xs, wu_ref[0].astype(BF16)) + bu_ref[0]
        gate = jnp.minimum(gate, SWIGLU_LIMIT)
        up = jnp.clip(up, -SWIGLU_LIMIT, SWIGLU_LIMIT)
        glu = gate * jax.nn.sigmoid(gate * SWIGLU_ALPHA)
        acc[rows, :] += _dot(((up + 1) * glu).astype(BF16), wd_ref[0].astype(BF16))
        if with_prefetch:
            prefetch(PREFETCH_PER_SUB * n_sub)
        for t in range(n_sub):
            @pl.when(last_tile)
            def _():
                out_copy(s0 + t).start()

    @pl.when(ns > 0)
    def _():
        def main(c, carry):
            batch_fits = pf_ref[0] + PREFETCH_PER_SUB * MAIN_SUBS <= next_rows
            lax.cond(batch_fits,
                     lambda: chunk(c * MAIN_SUBS, MAIN_SUBS, True),
                     lambda: chunk(c * MAIN_SUBS, MAIN_SUBS, False))
            return carry

        lax.fori_loop(0, ns // MAIN_SUBS, main, 0)
        for rem in range(1, MAIN_SUBS):
            @pl.when(ns % MAIN_SUBS == rem)
            def _():
                chunk(ns - rem, rem, False)

        @pl.when(last_tile)
        def _():
            def wait(s, carry):
                out_copy(s).wait()
                return carry

            lax.fori_loop(0, ns, wait, 0)

    @pl.when(jnp.logical_and(g == N_GROUPS - 1, j == N_FF_TILES - 1))
    def _():
        acc[0:SUB, :] = jnp.zeros((SUB, D_MODEL), F32)
        tail0 = tail_ref[0]
        n_tail = (N_ROWS - tail0) // SUB

        def tail_copy(s):
            dst = pl.multiple_of(tail0 + s * SUB, SUB)
            return pltpu.make_async_copy(acc.at[0:SUB, :], out_hbm.at[pl.ds(dst, SUB), :], osem)

        def start(s, carry):
            tail_copy(s).start()
            return carry

        def wait(s, carry):
            tail_copy(s).wait()
            return carry

        lax.fori_loop(0, n_tail, start, 0)
        lax.fori_loop(0, n_tail, wait, 0)


def _moe(layer, h2, plan, w_gu, b_gu, w_down, b_down):
    ge, gr, gn, tok, tail = plan
    nj = N_FF_TILES

    def jj(g, j, gn_ref):
        return jnp.where(gn_ref[g] > 0, j, nj - 1)

    sq = pl.Squeezed()
    grid_spec = pltpu.PrefetchScalarGridSpec(
        num_scalar_prefetch=5,
        grid=(N_GROUPS, nj),
        in_specs=[
            pl.BlockSpec(memory_space=pl.ANY),
            pl.BlockSpec((sq, 1, D_MODEL, FF_TILE),
                         lambda g, j, ge, gr, gn, *_: (layer, ge[g], 0, jj(g, j, gn))),
            pl.BlockSpec((sq, 1, D_MODEL, FF_TILE),
                         lambda g, j, ge, gr, gn, *_: (layer, ge[g], 0, nj + jj(g, j, gn))),
            pl.BlockSpec((sq, 1, FF_TILE, D_MODEL),
                         lambda g, j, ge, gr, gn, *_: (layer, ge[g], jj(g, j, gn), 0)),
            pl.BlockSpec((sq, 1, 1, FF_TILE),
                         lambda g, j, ge, gr, gn, *_: (layer, ge[g], 0, jj(g, j, gn))),
            pl.BlockSpec((sq, 1, 1, FF_TILE),
                         lambda g, j, ge, gr, gn, *_: (layer, ge[g], 0, nj + jj(g, j, gn))),
            pl.BlockSpec((sq, 1, 1, D_MODEL),
                         lambda g, j, ge, gr, gn, *_: (layer, ge[g], 0, 0)),
        ],
        out_specs=pl.BlockSpec(memory_space=pl.ANY),
        scratch_shapes=[
            pltpu.VMEM((GROUP_SUBS * SUB, D_MODEL), BF16),
            pltpu.VMEM((GROUP_SUBS * SUB // SUBLANES, SUBLANES, D_MODEL), F32),
            pltpu.VMEM((GROUP_SUBS * SUB, D_MODEL), F32),
            pltpu.SMEM((1,), I32),
            pltpu.SemaphoreType.DMA(()),
            pltpu.SemaphoreType.DMA(()),
        ],
    )
    return pl.pallas_call(
        _moe_kernel,
        grid_spec=grid_spec,
        out_shape=jax.ShapeDtypeStruct((N_ROWS, D_MODEL), F32),
        compiler_params=_cparams(2, MOE_VMEM_LIMIT),
        name="moe",
    )(ge, gr, gn, tok, tail, h2, w_gu, w_gu, w_down, b_gu, b_gu, b_down)


def _plan(idx, rank, cnt):
    idx = idx[:, :TOP_K]
    rank = rank[:, :TOP_K]
    counts = cnt.reshape(N_EXPERTS).astype(I32)
    ntile = (counts + SUB - 1) // SUB
    start = (jnp.cumsum(ntile) - ntile) * SUB
    onehot = idx[..., None] == jnp.arange(N_EXPERTS, dtype=I32)
    pos = jnp.sum(jnp.where(onehot, start, 0), axis=-1) + rank
    pos = pos.reshape(N_SLOTS).astype(I32)
    tok = jnp.zeros((N_ROWS,), I32).at[pos].set(
        jnp.repeat(jnp.arange(N_TOK, dtype=I32), TOP_K), unique_indices=True)
    ngrp = (ntile + GROUP_SUBS - 1) // GROUP_SUBS
    gcum = jnp.cumsum(ngrp)
    total = gcum[-1]
    g = jnp.arange(N_GROUPS, dtype=I32)
    e_g = jnp.sum((gcum[None, :] <= g[:, None]).astype(I32), axis=1)
    e_g = jnp.minimum(e_g, N_EXPERTS - 1)
    local = g - (gcum - ngrp)[e_g]
    valid = g < total
    nsub = jnp.where(valid, jnp.minimum(GROUP_SUBS, ntile[e_g] - local * GROUP_SUBS), 0)
    row0 = jnp.where(valid, start[e_g] + local * GROUP_SUBS * SUB, 0)
    e_last = e_g[jnp.maximum(total - 1, 0)]
    e_g = jnp.where(valid, e_g, e_last)
    tail = (jnp.sum(ntile) * SUB).astype(I32).reshape(1)
    return pos, (e_g.astype(I32), row0.astype(I32), nsub.astype(I32), tok, tail)


COMB_TM = 256


def _combine_kernel(pos_ref, x1_ref, mod_ref, wgt_ref, y_hbm, fg_ref, o_ref, buf, sem, *, final):
    i = pl.program_id(0)
    n = pl.num_programs(0)
    tm = COMB_TM

    def row_copy(p, slot, k, c, u):
        return pltpu.make_async_copy(y_hbm.at[pl.ds(p, 1), :],
                                     buf.at[slot, k, c, pl.ds(u, 1), :], sem.at[slot])

    def issue(tile, slot):
        base = tile * (tm * TOP_K)

        def body(c, carry):
            for u in range(SUBLANES):
                for k in range(TOP_K):
                    p = pos_ref[base + (c * SUBLANES + u) * TOP_K + k]
                    row_copy(p, slot, k, c, u).start()
            return carry

        lax.fori_loop(0, tm // SUBLANES, body, 0)

    def wait(slot):
        def body(c, carry):
            for u in range(SUBLANES):
                for k in range(TOP_K):
                    row_copy(0, slot, k, c, u).wait()
            return carry

        lax.fori_loop(0, tm // SUBLANES, body, 0)

    @pl.when(i == 0)
    def _():
        issue(0, 0)

    @pl.when(i + 1 < n)
    def _():
        issue(i + 1, (i + 1) % 2)

    slot = i % 2
    wait(slot)
    w = wgt_ref[...]
    y = w[:, 0:1] * buf[slot, 0].reshape(tm, D_MODEL)
    for k in range(1, TOP_K):
        y = y + w[:, k:k + 1] * buf[slot, k].reshape(tm, D_MODEL)
    x2 = x1_ref[...] + mod_ref[0, 5:6, :] * y
    if final:
        o_ref[...] = _rms_scale(x2) * fg_ref[...]
    else:
        o_ref[...] = x2


def _combine(pos, x1, mod, wgt, y_sorted, final_g, final):
    tm = COMB_TM
    grid_spec = pltpu.PrefetchScalarGridSpec(
        num_scalar_prefetch=1,
        grid=(N_TOK // tm,),
        in_specs=[
            pl.BlockSpec((tm, D_MODEL), lambda i, pos: (i, 0)),
            pl.BlockSpec((1, N_ADA, D_MODEL), lambda i, pos: (i * tm // SEQ, 0, 0)),
            pl.BlockSpec((tm, LANES), lambda i, pos: (i, 0)),
            pl.BlockSpec(memory_space=pl.ANY),
            pl.BlockSpec((1, D_MODEL), lambda i, pos: (0, 0)),
        ],
        out_specs=pl.BlockSpec((tm, D_MODEL), lambda i, pos: (i, 0)),
        scratch_shapes=[
            pltpu.VMEM((2, TOP_K, tm // SUBLANES, SUBLANES, D_MODEL), F32),
            pltpu.SemaphoreType.DMA((2,)),
        ],
    )
    return pl.pallas_call(
        functools.partial(_combine_kernel, final=final),
        grid_spec=grid_spec,
        out_shape=jax.ShapeDtypeStruct((N_TOK, D_MODEL), F32),
        compiler_params=_cparams(1),
        name="combine_final" if final else "combine",
    )(pos, x1, mod, wgt, y_sorted, final_g.reshape(1, D_MODEL))


QKV_TM = 512
QKV_TN = 1024
QKV_NC = D_MODEL // QKV_TN


def _qkv_kernel(x_ref, mod_ref, ng_ref, kvg_ref, w_ref, c_ref, s1_ref, s2_ref, o_ref, h_scr):
    j = pl.program_id(1)
    which = j // QKV_NC

    @pl.when(j == 0)
    def _():
        xn = _rms_scale(x_ref[...])
        hq = (xn * ng_ref[0:1, :]) * (1 + mod_ref[0, 1:2, :]) + mod_ref[0, 0:1, :]
        h_scr[0] = hq.astype(BF16)
        h_scr[1] = (xn * kvg_ref[...]).astype(BF16)

    z = _dot(h_scr[jnp.minimum(which, 1)], w_ref[...])
    is_rot = which < 2
    scale = jnp.where(which == 0, HEAD_DIM ** -0.5, 1.0).astype(F32)
    cos = jnp.where(is_rot, c_ref[...], 1.0)
    s1 = jnp.where(is_rot, s1_ref[...], 0.0)
    s2 = jnp.where(is_rot, s2_ref[...], 0.0)
    for hh in range(QKV_TN // HEAD_DIM):
        sl = slice(hh * HEAD_DIM, (hh + 1) * HEAD_DIM)
        zz = z[:, sl]
        rot = (zz * cos + pltpu.roll(zz, HEAD_DIM - ROT_HALF, 1) * s1
               + pltpu.roll(zz, ROT_HALF, 1) * s2)
        o_ref[:, sl] = (rot * scale).astype(BF16)


def _qkv(x2, mod, ng, kvg, w_qkv, cos_t, sin1_t, sin2_t):
    tm, tn = QKV_TM, QKV_TN
    row = lambda i, j: (i, 0)
    fixed = lambda i, j: (0, 0)
    return pl.pallas_call(
        _qkv_kernel,
        grid=(N_TOK // tm, 3 * QKV_NC),
        in_specs=[
            pl.BlockSpec((tm, D_MODEL), row),
            pl.BlockSpec((1, N_ADA, D_MODEL), lambda i, j: (i * tm // SEQ, 0, 0)),
            pl.BlockSpec((2, D_MODEL), fixed),
            pl.BlockSpec((1, D_MODEL), fixed),
            pl.BlockSpec((D_MODEL, tn), lambda i, j: (0, j)),
            pl.BlockSpec((tm, HEAD_DIM), row),
            pl.BlockSpec((tm, HEAD_DIM), row),
            pl.BlockSpec((tm, HEAD_DIM), row),
        ],
        out_specs=pl.BlockSpec((tm, tn), lambda i, j: (i, j)),
        out_shape=jax.ShapeDtypeStruct((N_TOK, 3 * D_MODEL), BF16),
        scratch_shapes=[pltpu.VMEM((2, tm, D_MODEL), BF16)],
        compiler_params=_cparams(2),
        name="qkv",
    )(x2, mod, ng, kvg.reshape(1, D_MODEL), w_qkv, cos_t, sin1_t, sin2_t)


def _attn_kernel(q_ref, k_ref, v_ref, o_ref, qaug_scr, kaug_scr, s_scr, p_scr, oacc_scr, km_scr):
    blk = MOBA_BLOCK
    kaug_scr[:, :HEAD_DIM] = k_ref[...]
    key_blk = lax.broadcasted_iota(I32, (SEQ, LANES), 0) >> 8
    key_lane = lax.broadcasted_iota(I32, (SEQ, LANES), 1)
    kaug_scr[:, HEAD_DIM:] = jnp.where(key_blk == key_lane, 1.0, 0.0).astype(BF16)
    km_scr[...] = jnp.zeros_like(km_scr)
    for jb in range(N_BLOCKS):
        kb = k_ref[jb * blk:(jb + 1) * blk, :].astype(F32)
        km_scr[jb:jb + 1, :] = jnp.mean(kb, axis=0, keepdims=True)
    km = km_scr[...].astype(BF16)

    q_all = q_ref[...]
    gate_t = lax.dot_general(km, q_all, NT_DIMS, preferred_element_type=F32)
    cand = lax.broadcasted_iota(I32, (SUBLANES, SEQ), 0)
    own = lax.broadcasted_iota(I32, (SUBLANES, SEQ), 1) >> 8
    past = cand < own
    rem = jnp.where(past, gate_t[0:N_BLOCKS, :], NEG_INF)
    keep = jnp.where(cand == own, 1.0, 0.0)
    for _ in range(MOBA_TOPK):
        m = jnp.max(rem, axis=0, keepdims=True)
        first = jnp.min(jnp.where(rem == m, cand, N_BLOCKS), axis=0, keepdims=True)
        hit = cand == first
        keep = jnp.where(jnp.logical_and(hit, past), 1.0, keep)
        rem = jnp.where(hit, -jnp.inf, rem)
    bias_t = jnp.concatenate([jnp.where(keep > 0.5, 0.0, NEG_INF),
                              jnp.zeros((LANES - SUBLANES, SEQ), F32)], axis=0)
    qaug_scr[:, :HEAD_DIM] = q_all
    qaug_scr[:, HEAD_DIM:] = bias_t.T.astype(BF16)

    for jb in range(N_BLOCKS):
        r0 = jb * blk
        s_scr[r0:, r0:r0 + blk] = lax.dot_general(
            qaug_scr[r0:, :], kaug_scr[r0:r0 + blk, :], NT_DIMS, preferred_element_type=F32)

    row = lax.broadcasted_iota(I32, (blk, blk), 0)
    col = lax.broadcasted_iota(I32, (blk, blk), 1)
    denom = []
    for qi in range(N_BLOCKS):
        q0 = qi * blk
        s_own = jnp.where(col <= row, s_scr[q0:q0 + blk, q0:q0 + blk], NEG_INF)
        m = jnp.max(s_own, axis=-1, keepdims=True)
        if qi > 0:
            s_past = s_scr[q0:q0 + blk, 0:q0]
            m = jnp.maximum(m, jnp.max(s_past, axis=-1, keepdims=True))
        p_own = jnp.exp(s_own - m)
        l = jnp.sum(p_own, axis=-1, keepdims=True)
        p_scr[q0:q0 + blk, q0:q0 + blk] = p_own.astype(BF16)
        if qi > 0:
            p_past = jnp.exp(s_past - m)
            l = l + jnp.sum(p_past, axis=-1, keepdims=True)
            p_scr[q0:q0 + blk, 0:q0] = p_past.astype(BF16)
        denom.append(l)

    for jb in range(N_BLOCKS):
        r0 = jb * blk
        pv = _dot(p_scr[r0:, r0:r0 + blk], v_ref[r0:r0 + blk, :])
        if jb == 0:
            oacc_scr[...] = pv
        else:
            oacc_scr[r0:, :] += pv
    for qi in range(N_BLOCKS):
        q0 = qi * blk
        o_ref[q0:q0 + blk, :] = (oacc_scr[q0:q0 + blk, :] / denom[qi]).astype(BF16)


def _attention(qkv):
    return pl.pallas_call(
        _attn_kernel,
        grid=(BATCH, N_HEADS),
        in_specs=[
            pl.BlockSpec((SEQ, HEAD_DIM), lambda b, h: (b, h)),
            pl.BlockSpec((SEQ, HEAD_DIM), lambda b, h: (b, N_HEADS + h)),
            pl.BlockSpec((SEQ, HEAD_DIM), lambda b, h: (b, 2 * N_HEADS + h)),
        ],
        out_specs=pl.BlockSpec((SEQ, HEAD_DIM), lambda b, h: (b, h)),
        out_shape=jax.ShapeDtypeStruct((N_TOK, D_MODEL), BF16),
        scratch_shapes=[pltpu.VMEM((SEQ, 2 * HEAD_DIM), BF16),
                        pltpu.VMEM((SEQ, 2 * HEAD_DIM), BF16),
                        pltpu.VMEM((SEQ, SEQ), F32),
                        pltpu.VMEM((SEQ, SEQ), BF16),
                        pltpu.VMEM((SEQ, HEAD_DIM), F32),
                        pltpu.VMEM((LANES, HEAD_DIM), F32)],
        compiler_params=_cparams(2),
        name="moba_attention",
    )(qkv, qkv, qkv)


OPROJ_TM = 512


def _oproj_kernel(a_ref, x_ref, mod_ref, ng_ref, w_ref, rw_ref, rb_ref,
                  x1_ref, h2_ref, idx_ref, wgt_ref, rank_ref, cnt_ref, cnt_scr):
    i = pl.program_id(0)

    @pl.when(i == 0)
    def _():
        cnt_scr[...] = jnp.zeros_like(cnt_scr)

    x1 = x_ref[...] + mod_ref[0, 2:3, :] * _dot(a_ref[...], w_ref[...])
    x1_ref[...] = x1
    _route(x1, i, ng_ref, mod_ref, rw_ref, rb_ref,
           h2_ref, idx_ref, wgt_ref, rank_ref, cnt_ref, cnt_scr)


def _oproj(attn, x2, mod, ng, w_o, rw, rb):
    tm = OPROJ_TM
    row = lambda i: (i, 0)
    fixed = lambda i: (0, 0)
    return pl.pallas_call(
        _oproj_kernel,
        grid=(N_TOK // tm,),
        in_specs=[
            pl.BlockSpec((tm, D_MODEL), row),
            pl.BlockSpec((tm, D_MODEL), row),
            pl.BlockSpec((1, N_ADA, D_MODEL), lambda i: (i * tm // SEQ, 0, 0)),
            pl.BlockSpec((2, D_MODEL), fixed),
            pl.BlockSpec((D_MODEL, D_MODEL), fixed, pipeline_mode=pl.Buffered(1)),
            pl.BlockSpec((D_MODEL, 2 * N_EXPERTS), fixed),
            pl.BlockSpec((1, N_EXPERTS), fixed),
        ],
        out_specs=(
            pl.BlockSpec((tm, D_MODEL), row),
            pl.BlockSpec((tm, D_MODEL), row),
            pl.BlockSpec((tm, LANES), row),
            pl.BlockSpec((tm, LANES), row),
            pl.BlockSpec((tm, LANES), row),
            pl.BlockSpec((1, N_EXPERTS), fixed),
        ),
        out_shape=_route_out_shapes(),
        scratch_shapes=[pltpu.VMEM((1, N_EXPERTS), F32)],
        compiler_params=_cparams(1),
        name="oproj",
    )(attn, x2, mod, ng, w_o, rw, rb)


def _split_hi_lo(w):
    hi = w.astype(BF16)
    lo = (w - hi.astype(F32)).astype(BF16)
    return jnp.concatenate([hi, lo], axis=1)


def _rope_tables(positions):
    inv = jnp.float32(ROPE_THETA) ** (-jnp.arange(0, ROT_DIM, 2, dtype=F32) / ROT_DIM)
    ang = positions.astype(F32).reshape(N_TOK, 1) * inv
    cos, sin = jnp.cos(ang), jnp.sin(ang)
    rest = HEAD_DIM - ROT_DIM
    cos_t = jnp.concatenate([cos, cos, jnp.ones((N_TOK, rest), F32)], axis=1)
    sin1_t = jnp.concatenate([-sin, jnp.zeros((N_TOK, HEAD_DIM - ROT_HALF), F32)], axis=1)
    sin2_t = jnp.concatenate([jnp.zeros((N_TOK, ROT_HALF), F32), sin,
                              jnp.zeros((N_TOK, rest), F32)], axis=1)
    return cos_t, sin1_t, sin2_t


def _moe_block(layer, x1, h2, idx, wgt, rank, cnt, mod, moe_w_gu, moe_b_gu, moe_w_down,
               moe_b_down, final_g, final):
    pos, plan = _plan(idx, rank, cnt)
    depth = moe_w_gu.shape[0]
    y_sorted = _moe(layer, h2, plan, moe_w_gu,
                    moe_b_gu.reshape(depth, N_EXPERTS, 1, 2 * D_FF), moe_w_down,
                    moe_b_down.reshape(depth, N_EXPERTS, 1, D_MODEL))
    return _combine(pos, x1, mod, wgt, y_sorted, final_g, final)


def kernel(x, c, positions, norm_g, ada_w, ada_b, conv_w_in, conv_w, conv_w_out, kv_norm_g, w_kv,
           attn_w_q, attn_w_o, router_w, router_b, moe_w_gu, moe_b_gu, moe_w_down, moe_b_down,
           final_g):
    xf = x.reshape(N_TOK, D_MODEL)
    ada = _ada(c, ada_w, ada_b)
    mod = ada[:, :BATCH].reshape(2, BATCH, N_ADA, D_MODEL)
    cos_t, sin1_t, sin2_t = _rope_tables(positions)

    x1, h2, idx, wgt, rank, cnt = _mixer0(
        xf, mod[0], norm_g[0], conv_w_in[0].astype(BF16), conv_w[0], conv_w_out[0].astype(BF16),
        _split_hi_lo(router_w[0]), router_b[0].reshape(1, N_EXPERTS))
    x2 = _moe_block(0, x1, h2, idx, wgt, rank, cnt, mod[0], moe_w_gu, moe_b_gu, moe_w_down,
                    moe_b_down, final_g, False)

    w_qkv = jnp.concatenate([attn_w_q[0], w_kv], axis=1).astype(BF16)
    qkv = _qkv(x2, mod[1], norm_g[1], kv_norm_g, w_qkv, cos_t, sin1_t, sin2_t)
    attn = _attention(qkv)
    x3, h2b, idx, wgt, rank, cnt = _oproj(
        attn, x2, mod[1], norm_g[1], attn_w_o[0].astype(BF16), _split_hi_lo(router_w[1]),
        router_b[1].reshape(1, N_EXPERTS))
    out = _moe_block(1, x3, h2b, idx, wgt, rank, cnt, mod[1], moe_w_gu, moe_b_gu, moe_w_down,
                     moe_b_down, final_g, True)
    return out.reshape(BATCH, SEQ, D_MODEL)
```

```python
import jax
import jax.numpy as jnp
from jax import lax
from jax.experimental import pallas as pl
from jax.experimental.pallas import tpu as pltpu

F32 = jnp.float32
BF16 = jnp.bfloat16
I32 = jnp.int32

D_MODEL = 2048
BATCH = 4
SEQ = 2048
N_TOK = BATCH * SEQ
N_HEADS = 16
HEAD_DIM = 128
ROT_DIM = 32
ROT_HALF = ROT_DIM // 2
ROPE_THETA = 500000.0
MOBA_BLOCK = 256
MOBA_TOPK = 3
N_BLOCKS = SEQ // MOBA_BLOCK
N_EXPERTS = 32
TOP_K = 4
D_FF = D_MODEL
SWIGLU_ALPHA = 1.702
SWIGLU_LIMIT = 7.0
NORM_EPS = 1e-5
NEG_INF = -1e30
N_ADA = 6

LANES = 128
SUBLANES = 8
VMEM_LIMIT = 56 * 1024 * 1024
MOE_VMEM_LIMIT = 60 * 1024 * 1024

SUB = 128
GROUP_SUBS = 16
MAIN_SUBS = 4
PREFETCH_PER_SUB = 32
N_SLOTS = N_TOK * TOP_K
N_ROWS = N_SLOTS + N_EXPERTS * SUB
N_GROUPS = N_EXPERTS + N_SLOTS // (SUB * GROUP_SUBS)
FF_TILE = 256
N_FF_TILES = D_FF // FF_TILE

NT_DIMS = (((1,), (1,)), ((), ()))


def _cparams(n_axes, vmem_limit=VMEM_LIMIT):
    return pltpu.CompilerParams(dimension_semantics=("arbitrary",) * n_axes,
                                vmem_limit_bytes=vmem_limit)


def _rms_scale(x):
    return x * lax.rsqrt(jnp.mean(x * x, axis=-1, keepdims=True) + NORM_EPS)


def _modulate(x, g, shift, scale):
    return (_rms_scale(x) * g) * (1 + scale) + shift


def _dot(a, b):
    return jnp.dot(a, b, preferred_element_type=F32)


ADA_TN = 1024


def _ada_kernel(c_ref, w_ref, b_ref, o_ref):
    c = c_ref[...]
    c_act = c * jax.nn.sigmoid(c)
    o_ref[0] = _dot(c_act.astype(BF16), w_ref[0].astype(BF16)) + b_ref[0]


def _ada(c, ada_w, ada_b):
    depth = ada_w.shape[0]
    n_out = ada_w.shape[2]
    c_pad = jnp.zeros((8, D_MODEL), F32).at[:BATCH].set(c)
    return pl.pallas_call(
        _ada_kernel,
        grid=(depth, n_out // ADA_TN),
        in_specs=[
            pl.BlockSpec((8, D_MODEL), lambda l, j: (0, 0)),
            pl.BlockSpec((1, D_MODEL, ADA_TN), lambda l, j: (l, 0, j)),
            pl.BlockSpec((1, 1, ADA_TN), lambda l, j: (l, 0, j)),
        ],
        out_specs=pl.BlockSpec((1, 8, ADA_TN), lambda l, j: (l, 0, j)),
        out_shape=jax.ShapeDtypeStruct((depth, 8, n_out), F32),
        compiler_params=_cparams(2),
        name="ada",
    )(c_pad, ada_w, ada_b.reshape(depth, 1, n_out))


def _route(x1, i, ng_ref, mod_ref, rw_ref, rb_ref,
           h2_ref, idx_ref, wgt_ref, rank_ref, cnt_ref, cnt_scr):
    tm = x1.shape[0]
    h2 = _modulate(x1, ng_ref[1:2, :], mod_ref[0, 3:4, :], mod_ref[0, 4:5, :])
    h2_ref[...] = h2
    h_hi = h2.astype(BF16)
    h_lo = (h2 - h_hi.astype(F32)).astype(BF16)
    r_hi = _dot(h_hi, rw_ref[...])
    r_lo = _dot(h_lo, rw_ref[...])
    logits = (r_hi[:, :N_EXPERTS] + r_hi[:, N_EXPERTS:] + r_lo[:, :N_EXPERTS]) + rb_ref[...]
    lane = lax.broadcasted_iota(I32, (tm, N_EXPERTS), 1)
    vals, idxs = [], []
    rem = logits
    for _ in range(TOP_K):
        m = jnp.max(rem, axis=-1, keepdims=True)
        ik = jnp.min(jnp.where(rem == m, lane, N_EXPERTS), axis=-1, keepdims=True)
        vals.append(m)
        idxs.append(ik)
        rem = jnp.where(lane == ik, -jnp.inf, rem)
    exps = [jnp.exp(v - vals[0]) for v in vals]
    den = exps[0] + exps[1] + exps[2] + exps[3]
    hot = jnp.zeros((tm, N_EXPERTS), F32)
    for ik in idxs:
        hot = jnp.where(lane == ik, 1.0, hot)
    row = lax.broadcasted_iota(I32, (tm, tm), 0)
    col = lax.broadcasted_iota(I32, (tm, tm), 1)
    tri = jnp.where(col < row, 1.0, 0.0).astype(BF16)
    prior = cnt_scr[...]
    rank_mat = _dot(tri, hot.astype(BF16)) + prior
    cnt_new = prior + jnp.sum(hot, axis=0, keepdims=True)
    cnt_scr[...] = cnt_new
    cnt_ref[...] = cnt_new
    lane_o = lax.broadcasted_iota(I32, (tm, LANES), 1)
    idx_o = jnp.zeros((tm, LANES), I32)
    wgt_o = jnp.zeros((tm, LANES), F32)
    rank_o = jnp.zeros((tm, LANES), I32)
    for k in range(TOP_K):
        rk = jnp.sum(jnp.where(lane == idxs[k], rank_mat, 0.0), axis=-1, keepdims=True)
        idx_o = jnp.where(lane_o == k, idxs[k], idx_o)
        wgt_o = jnp.where(lane_o == k, exps[k] / den, wgt_o)
        rank_o = jnp.where(lane_o == k, rk.astype(I32), rank_o)
    idx_ref[...] = idx_o
    wgt_ref[...] = wgt_o
    rank_ref[...] = rank_o


def _route_out_shapes():
    return (
        jax.ShapeDtypeStruct((N_TOK, D_MODEL), F32),
        jax.ShapeDtypeStruct((N_TOK, D_MODEL), F32),
        jax.ShapeDtypeStruct((N_TOK, LANES), I32),
        jax.ShapeDtypeStruct((N_TOK, LANES), F32),
        jax.ShapeDtypeStruct((N_TOK, LANES), I32),
        jax.ShapeDtypeStruct((1, N_EXPERTS), F32),
    )


MIX_TM = 512
MIX_TN = 512
MIX_NC = D_MODEL // MIX_TN


def _mixer0_kernel(x_ref, mod_ref, ng_ref, wb_ref, wc_ref, wu_ref, cw_ref, wo_ref, rw_ref, rb_ref,
                   x1_ref, h2_ref, idx_ref, wgt_ref, rank_ref, cnt_ref,
                   h_scr, vpad_scr, carry_scr, cnt_scr):
    i = pl.program_id(0)
    j = pl.program_id(1)
    tm = MIX_TM

    @pl.when(jnp.logical_and(i == 0, j == 0))
    def _():
        cnt_scr[...] = jnp.zeros_like(cnt_scr)
        carry_scr[...] = jnp.zeros_like(carry_scr)

    @pl.when(j == 0)
    def _():
        h = _modulate(x_ref[...], ng_ref[0:1, :], mod_ref[0, 0:1, :], mod_ref[0, 1:2, :])
        h_scr[...] = h.astype(BF16)
        x1_ref[...] = jnp.zeros_like(x1_ref)

    h = h_scr[...]
    b_gate = _dot(h, wb_ref[...])
    v = _dot(h, wc_ref[...]) * _dot(h, wu_ref[...])
    seq_start = (i % (SEQ // tm)) == 0
    vpad_scr[0:8, :] = jnp.where(seq_start, 0.0, carry_scr[j])
    vpad_scr[8:8 + tm, :] = v
    carry_scr[j] = v[tm - 8:tm, :]
    cw = cw_ref[...]
    conv = (cw[0:1, :] * vpad_scr[6:6 + tm, :] + cw[1:2, :] * vpad_scr[7:7 + tm, :]
            + cw[2:3, :] * v)
    x1_ref[...] += _dot((b_gate * conv).astype(BF16), wo_ref[...])

    @pl.when(j == MIX_NC - 1)
    def _():
        x1 = x_ref[...] + mod_ref[0, 2:3, :] * x1_ref[...]
        x1_ref[...] = x1
        _route(x1, i, ng_ref, mod_ref, rw_ref, rb_ref,
               h2_ref, idx_ref, wgt_ref, rank_ref, cnt_ref, cnt_scr)


def _mixer0(x, mod, ng, w_in, cw, w_out, rw, rb):
    tm, tn, nc = MIX_TM, MIX_TN, MIX_NC
    row = lambda i, j: (i, 0)
    fixed = lambda i, j: (0, 0)
    return pl.pallas_call(
        _mixer0_kernel,
        grid=(N_TOK // tm, nc),
        in_specs=[
            pl.BlockSpec((tm, D_MODEL), row),
            pl.BlockSpec((1, N_ADA, D_MODEL), lambda i, j: (i * tm // SEQ, 0, 0)),
            pl.BlockSpec((2, D_MODEL), fixed),
            pl.BlockSpec((D_MODEL, tn), lambda i, j: (0, j)),
            pl.BlockSpec((D_MODEL, tn), lambda i, j: (0, nc + j)),
            pl.BlockSpec((D_MODEL, tn), lambda i, j: (0, 2 * nc + j)),
            pl.BlockSpec((3, tn), lambda i, j: (0, j)),
            pl.BlockSpec((tn, D_MODEL), lambda i, j: (j, 0)),
            pl.BlockSpec((D_MODEL, 2 * N_EXPERTS), fixed),
            pl.BlockSpec((1, N_EXPERTS), fixed),
        ],
        out_specs=(
            pl.BlockSpec((tm, D_MODEL), row),
            pl.BlockSpec((tm, D_MODEL), row),
            pl.BlockSpec((tm, LANES), row),
            pl.BlockSpec((tm, LANES), row),
            pl.BlockSpec((tm, LANES), row),
            pl.BlockSpec((1, N_EXPERTS), fixed),
        ),
        out_shape=_route_out_shapes(),
        scratch_shapes=[
            pltpu.VMEM((tm, D_MODEL), BF16),
            pltpu.VMEM((tm + 8, tn), F32),
            pltpu.VMEM((nc, 8, tn), F32),
            pltpu.VMEM((1, N_EXPERTS), F32),
        ],
        compiler_params=_cparams(2),
        name="mixer0",
    )(x, mod, ng, w_in, w_in, w_in, cw, w_out, rw, rb)


def _moe_kernel(ge_ref, gr_ref, gn_ref, tok_ref, tail_ref,
                h_hbm, wg_ref, wu_ref, wd_ref, bg_ref, bu_ref, bd_ref,
                out_hbm,
                xg, stage, acc, pf_ref, gsem, osem):
    g = pl.program_id(0)
    j = pl.program_id(1)
    ns = gn_ref[g]
    row0 = gr_ref[g]
    n_rows = ns * SUB
    g_next = jnp.minimum(g + 1, N_GROUPS - 1)
    next_row0 = gr_ref[g_next]
    next_rows = jnp.where(g + 1 < N_GROUPS, gn_ref[g_next], 0) * SUB

    def row_copy(tok, c, u):
        return pltpu.make_async_copy(h_hbm.at[pl.ds(tok, 1), :],
                                     stage.at[c, pl.ds(u, 1), :], gsem)

    def sub_rows(s):
        return pl.ds(pl.multiple_of(s * SUB, SUB), SUB)

    @pl.when(jnp.logical_and(g == 0, j == 0))
    def _():
        pf_ref[0] = 0

    @pl.when(jnp.logical_and(j == 0, ns > 0))
    def _():
        covered = jnp.minimum(pf_ref[0], n_rows)

        def fetch(c, carry):
            for u in range(SUBLANES):
                row_copy(tok_ref[row0 + c * SUBLANES + u], c, u).start()
            return carry

        lax.fori_loop(covered // SUBLANES, n_rows // SUBLANES, fetch, 0)

        def wait(c, carry):
            for u in range(SUBLANES):
                row_copy(0, c, u).wait()
            return carry

        lax.fori_loop(0, n_rows // SUBLANES, wait, 0)
        bias = jnp.broadcast_to(bd_ref[0], (SUB, D_MODEL))

        def cast(s, carry):
            tiles = pl.ds(pl.multiple_of(s * (SUB // SUBLANES), SUB // SUBLANES), SUB // SUBLANES)
            xg[sub_rows(s), :] = stage[tiles].reshape(SUB, D_MODEL).astype(BF16)
            acc[sub_rows(s), :] = bias
            return carry

        lax.fori_loop(0, ns, cast, 0)
        pf_ref[0] = 0

    def prefetch(n_offer):
        ptr = pf_ref[0]
        c0 = ptr // SUBLANES
        for u in range(n_offer):
            tok = tok_ref[next_row0 + ptr + u]
            row_copy(tok, c0 + u // SUBLANES, u % SUBLANES).start()
        pf_ref[0] = ptr + n_offer

    def out_copy(s):
        dst = pl.multiple_of(row0 + s * SUB, SUB)
        return pltpu.make_async_copy(acc.at[sub_rows(s), :],
                                     out_hbm.at[pl.ds(dst, SUB), :], osem)

    last_tile = j == N_FF_TILES - 1

    def chunk(s0, n_sub, with_prefetch):
        rows = pl.ds(pl.multiple_of(s0 * SUB, SUB), n_sub * SUB)
        xs = xg[rows, :]
        gate = _dot(xs, wg_ref[0].astype(BF16)) + bg_ref[0]
        up = _dot(xs, wu_ref[0].astype(BF16)) + bu_ref[0]
        gate = jnp.minimum(gate, SWIGLU_LIMIT)
        up = jnp.clip(up, -SWIGLU_LIMIT, SWIGLU_LIMIT)
        glu = gate * jax.nn.sigmoid(gate * SWIGLU_ALPHA)
        acc[rows, :] += _dot(((up + 1) * glu).astype(BF16), wd_ref[0].astype(BF16))
        if with_prefetch:
            prefetch(PREFETCH_PER_SUB * n_sub)
        for t in range(n_sub):
            @pl.when(last_tile)
            def _():
                out_copy(s0 + t).start()

    @pl.when(ns > 0)
    def _():
        def main(c, carry):
            batch_fits = pf_ref[0] + PREFETCH_PER_SUB * MAIN_SUBS <= next_rows
            lax.cond(batch_fits,
                     lambda: chunk(c * MAIN_SUBS, MAIN_SUBS, True),
                     lambda: chunk(c * MAIN_SUBS, MAIN_SUBS, False))
            return carry

        lax.fori_loop(0, ns // MAIN_SUBS, main, 0)
        for rem in range(1, MAIN_SUBS):
            @pl.when(ns % MAIN_SUBS == rem)
            def _():
                chunk(ns - rem, rem, False)

        @pl.when(last_tile)
        def _():
            def wait(s, carry):
                out_copy(s).wait()
                return carry

            lax.fori_loop(0, ns, wait, 0)

    @pl.when(jnp.logical_and(g == N_GROUPS - 1, j == N_FF_TILES - 1))
    def _():
        acc[0:SUB, :] = jnp.zeros((SUB, D_MODEL), F32)
        tail0 = tail_ref[0]
        n_tail = (N_ROWS - tail0) // SUB

        def tail_copy(s):
            dst = pl.multiple_of(tail0 + s * SUB, SUB)
            return pltpu.make_async_copy(acc.at[0:SUB, :], out_hbm.at[pl.ds(dst, SUB), :], osem)

        def start(s, carry):
            tail_copy(s).start()
            return carry

        def wait(s, carry):
            tail_copy(s).wait()
            return carry

        lax.fori_loop(0, n_tail, start, 0)
        lax.fori_loop(0, n_tail, wait, 0)


def _moe(layer, h2, plan, w_gu, b_gu, w_down, b_down):
    ge, gr, gn, tok, tail = plan
    nj = N_FF_TILES

    def jj(g, j, gn_ref):
        return jnp.where(gn_ref[g] > 0, j, nj - 1)

    sq = pl.Squeezed()
    grid_spec = pltpu.PrefetchScalarGridSpec(
        num_scalar_prefetch=5,
        grid=(N_GROUPS, nj),
        in_specs=[
            pl.BlockSpec(memory_space=pl.ANY),
            pl.BlockSpec((sq, 1, D_MODEL, FF_TILE),
                         lambda g, j, ge, gr, gn, *_: (layer, ge[g], 0, jj(g, j, gn))),
            pl.BlockSpec((sq, 1, D_MODEL, FF_TILE),
                         lambda g, j, ge, gr, gn, *_: (layer, ge[g], 0, nj + jj(g, j, gn))),
            pl.BlockSpec((sq, 1, FF_TILE, D_MODEL),
                         lambda g, j, ge, gr, gn, *_: (layer, ge[g], jj(g, j, gn), 0)),
            pl.BlockSpec((sq, 1, 1, FF_TILE),
                         lambda g, j, ge, gr, gn, *_: (layer, ge[g], 0, jj(g, j, gn))),
            pl.BlockSpec((sq, 1, 1, FF_TILE),
                         lambda g, j, ge, gr, gn, *_: (layer, ge[g], 0, nj + jj(g, j, gn))),
            pl.BlockSpec((sq, 1, 1, D_MODEL),
                         lambda g, j, ge, gr, gn, *_: (layer, ge[g], 0, 0)),
        ],
        out_specs=pl.BlockSpec(memory_space=pl.ANY),
        scratch_shapes=[
            pltpu.VMEM((GROUP_SUBS * SUB, D_MODEL), BF16),
            pltpu.VMEM((GROUP_SUBS * SUB // SUBLANES, SUBLANES, D_MODEL), F32),
            pltpu.VMEM((GROUP_SUBS * SUB, D_MODEL), F32),
            pltpu.SMEM((1,), I32),
            pltpu.SemaphoreType.DMA(()),
            pltpu.SemaphoreType.DMA(()),
        ],
    )
    return pl.pallas_call(
        _moe_kernel,
        grid_spec=grid_spec,
        out_shape=jax.ShapeDtypeStruct((N_ROWS, D_MODEL), F32),
        compiler_params=_cparams(2, MOE_VMEM_LIMIT),
        name="moe",
    )(ge, gr, gn, tok, tail, h2, w_gu, w_gu, w_down, b_gu, b_gu, b_down)


def _plan(idx, rank, cnt):
    idx = idx[:, :TOP_K]
    rank = rank[:, :TOP_K]
    counts = cnt.reshape(N_EXPERTS).astype(I32)
    ntile = (counts + SUB - 1) // SUB
    start = (jnp.cumsum(ntile) - ntile) * SUB
    onehot = idx[..., None] == jnp.arange(N_EXPERTS, dtype=I32)
    pos = jnp.sum(jnp.where(onehot, start, 0), axis=-1) + rank
    pos = pos.reshape(N_SLOTS).astype(I32)
    tok = jnp.zeros((N_ROWS,), I32).at[pos].set(
        jnp.repeat(jnp.arange(N_TOK, dtype=I32), TOP_K), unique_indices=True)
    ngrp = (ntile + GROUP_SUBS - 1) // GROUP_SUBS
    gcum = jnp.cumsum(ngrp)
    total = gcum[-1]
    g = jnp.arange(N_GROUPS, dtype=I32)
    e_g = jnp.sum((gcum[None, :] <= g[:, None]).astype(I32), axis=1)
    e_g = jnp.minimum(e_g, N_EXPERTS - 1)
    local = g - (gcum - ngrp)[e_g]
    valid = g < total
    nsub = jnp.where(valid, jnp.minimum(GROUP_SUBS, ntile[e_g] - local * GROUP_SUBS), 0)
    row0 = jnp.where(valid, start[e_g] + local * GROUP_SUBS * SUB, 0)
    e_last = e_g[jnp.maximum(total - 1, 0)]
    e_g = jnp.where(valid, e_g, e_last)
    tail = (jnp.sum(ntile) * SUB).astype(I32).reshape(1)
    return pos, (e_g.astype(I32), row0.astype(I32), nsub.astype(I32), tok, tail)


COMB_TM = 256


def _combine_final_kernel(pos_ref, x1_ref, mod_ref, wgt_ref, y_hbm, fg_ref, o_ref, buf, sem):
    i = pl.program_id(0)
    n = pl.num_programs(0)
    tm = COMB_TM

    def row_copy(p, slot, k, c, u):
        return pltpu.make_async_copy(y_hbm.at[pl.ds(p, 1), :],
                                     buf.at[slot, k, c, pl.ds(u, 1), :], sem.at[slot])

    def issue(tile, slot):
        base = tile * (tm * TOP_K)

        def body(c, carry):
            for u in range(SUBLANES):
                for k in range(TOP_K):
                    p = pos_ref[base + (c * SUBLANES + u) * TOP_K + k]
                    row_copy(p, slot, k, c, u).start()
            return carry

        lax.fori_loop(0, tm // SUBLANES, body, 0)

    def wait(slot):
        def body(c, carry):
            for u in range(SUBLANES):
                for k in range(TOP_K):
                    row_copy(0, slot, k, c, u).wait()
            return carry

        lax.fori_loop(0, tm // SUBLANES, body, 0)

    @pl.when(i == 0)
    def _():
        issue(0, 0)

    @pl.when(i + 1 < n)
    def _():
        issue(i + 1, (i + 1) % 2)

    slot = i % 2
    wait(slot)
    w = wgt_ref[...]
    y = w[:, 0:1] * buf[slot, 0].reshape(tm, D_MODEL)
    for k in range(1, TOP_K):
        y = y + w[:, k:k + 1] * buf[slot, k].reshape(tm, D_MODEL)
    x2 = x1_ref[...] + mod_ref[0, 5:6, :] * y
    o_ref[...] = _rms_scale(x2) * fg_ref[...]


def _combine_final(pos, x1, mod, wgt, y_sorted, final_g):
    tm = COMB_TM
    grid_spec = pltpu.PrefetchScalarGridSpec(
        num_scalar_prefetch=1,
        grid=(N_TOK // tm,),
        in_specs=[
            pl.BlockSpec((tm, D_MODEL), lambda i, pos: (i, 0)),
            pl.BlockSpec((1, N_ADA, D_MODEL), lambda i, pos: (i * tm // SEQ, 0, 0)),
            pl.BlockSpec((tm, LANES), lambda i, pos: (i, 0)),
            pl.BlockSpec(memory_space=pl.ANY),
            pl.BlockSpec((1, D_MODEL), lambda i, pos: (0, 0)),
        ],
        out_specs=pl.BlockSpec((tm, D_MODEL), lambda i, pos: (i, 0)),
        scratch_shapes=[
            pltpu.VMEM((2, TOP_K, tm // SUBLANES, SUBLANES, D_MODEL), F32),
            pltpu.SemaphoreType.DMA((2,)),
        ],
    )
    return pl.pallas_call(
        _combine_final_kernel,
        grid_spec=grid_spec,
        out_shape=jax.ShapeDtypeStruct((N_TOK, D_MODEL), F32),
        compiler_params=_cparams(1),
        name="combine_final",
    )(pos, x1, mod, wgt, y_sorted, final_g.reshape(1, D_MODEL))


QKV_TM = 512
QKV_TN = 1024
QKV_NC = D_MODEL // QKV_TN


QKV_GATHER_ROWS = 128
QKV_GATHER_STEPS = QKV_TM // QKV_GATHER_ROWS


def _combine_qkv_kernel(pos_ref, x1_ref, mod0_ref, wgt_ref, y_hbm,
                        mod_ref, ng_ref, kvg_ref, w_ref, c_ref, s1_ref, s2_ref,
                        x2_ref, o_ref, h_scr, stage, sem):
    i = pl.program_id(0)
    j = pl.program_id(1)
    tm = QKV_TM
    which = j // QKV_NC

    def row_copy(p, k, c, u):
        return pltpu.make_async_copy(y_hbm.at[pl.ds(p, 1), :],
                                     stage.at[k, c, pl.ds(u, 1), :], sem)

    @pl.when(jnp.logical_and(i == 0, j == 0))
    def _():
        def fetch(c, carry):
            for u in range(SUBLANES):
                for k in range(TOP_K):
                    row_copy(pos_ref[(c * SUBLANES + u) * TOP_K + k], k, c, u).start()
            return carry

        lax.fori_loop(0, tm // SUBLANES, fetch, 0)

    @pl.when(j == 0)
    def _():
        def wait(c, carry):
            for u in range(SUBLANES):
                for k in range(TOP_K):
                    row_copy(0, k, c, u).wait()
            return carry

        lax.fori_loop(0, tm // SUBLANES, wait, 0)
        w = wgt_ref[...]
        y = w[:, 0:1] * stage[0].reshape(tm, D_MODEL)
        for k in range(1, TOP_K):
            y = y + w[:, k:k + 1] * stage[k].reshape(tm, D_MODEL)
        x2 = x1_ref[...] + mod0_ref[0, 5:6, :] * y
        x2_ref[...] = x2
        xn = _rms_scale(x2)
        hq = (xn * ng_ref[0:1, :]) * (1 + mod_ref[0, 1:2, :]) + mod_ref[0, 0:1, :]
        h_scr[0] = hq.astype(BF16)
        h_scr[1] = (xn * kvg_ref[...]).astype(BF16)

    def project():
        z = _dot(h_scr[jnp.minimum(which, 1)], w_ref[...])
        is_rot = which < 2
        scale = jnp.where(which == 0, HEAD_DIM ** -0.5, 1.0).astype(F32)
        cos = jnp.where(is_rot, c_ref[...], 1.0)
        s1 = jnp.where(is_rot, s1_ref[...], 0.0)
        s2 = jnp.where(is_rot, s2_ref[...], 0.0)
        for hh in range(QKV_TN // HEAD_DIM):
            sl = slice(hh * HEAD_DIM, (hh + 1) * HEAD_DIM)
            zz = z[:, sl]
            rot = (zz * cos + pltpu.roll(zz, HEAD_DIM - ROT_HALF, 1) * s1
                   + pltpu.roll(zz, ROT_HALF, 1) * s2)
            o_ref[:, sl] = (rot * scale).astype(BF16)

    def project_and_gather():
        base = ((i + 1) * tm + j * QKV_GATHER_ROWS) * TOP_K
        c0 = j * (QKV_GATHER_ROWS // SUBLANES)
        for u in range(QKV_GATHER_ROWS):
            for k in range(TOP_K):
                row_copy(pos_ref[base + u * TOP_K + k], k, c0 + u // SUBLANES,
                         u % SUBLANES).start()
        project()

    has_batch = jnp.logical_and(i + 1 < pl.num_programs(0), j < QKV_GATHER_STEPS)
    lax.cond(has_batch, project_and_gather, project)


def _combine_qkv(pos, x1, mod0, wgt, y_sorted, mod, ng, kvg, w_qkv, cos_t, sin1_t, sin2_t):
    tm, tn = QKV_TM, QKV_TN
    row = lambda i, j, pos: (i, 0)
    fixed = lambda i, j, pos: (0, 0)
    batch = lambda i, j, pos: (i * tm // SEQ, 0, 0)
    grid_spec = pltpu.PrefetchScalarGridSpec(
        num_scalar_prefetch=1,
        grid=(N_TOK // tm, 3 * QKV_NC),
        in_specs=[
            pl.BlockSpec((tm, D_MODEL), row),
            pl.BlockSpec((1, N_ADA, D_MODEL), batch),
            pl.BlockSpec((tm, LANES), row),
            pl.BlockSpec(memory_space=pl.ANY),
            pl.BlockSpec((1, N_ADA, D_MODEL), batch),
            pl.BlockSpec((2, D_MODEL), fixed),
            pl.BlockSpec((1, D_MODEL), fixed),
            pl.BlockSpec((D_MODEL, tn), lambda i, j, pos: (0, j)),
            pl.BlockSpec((tm, HEAD_DIM), row),
            pl.BlockSpec((tm, HEAD_DIM), row),
            pl.BlockSpec((tm, HEAD_DIM), row),
        ],
        out_specs=(
            pl.BlockSpec((tm, D_MODEL), row),
            pl.BlockSpec((tm, tn), lambda i, j, pos: (i, j)),
        ),
        scratch_shapes=[
            pltpu.VMEM((2, tm, D_MODEL), BF16),
            pltpu.VMEM((TOP_K, tm // SUBLANES, SUBLANES, D_MODEL), F32),
            pltpu.SemaphoreType.DMA(()),
        ],
    )
    return pl.pallas_call(
        _combine_qkv_kernel,
        grid_spec=grid_spec,
        out_shape=(jax.ShapeDtypeStruct((N_TOK, D_MODEL), F32),
                   jax.ShapeDtypeStruct((N_TOK, 3 * D_MODEL), BF16)),
        compiler_params=_cparams(2),
        name="combine_qkv",
    )(pos, x1, mod0, wgt, y_sorted, mod, ng, kvg.reshape(1, D_MODEL), w_qkv, cos_t, sin1_t, sin2_t)


def _attn_kernel(q_ref, k_ref, v_ref, o_ref, qaug_scr, kaug_scr, s_scr, p_scr, oacc_scr, km_scr):
    blk = MOBA_BLOCK
    kaug_scr[:, :HEAD_DIM] = k_ref[...]
    key_blk = lax.broadcasted_iota(I32, (SEQ, LANES), 0) >> 8
    key_lane = lax.broadcasted_iota(I32, (SEQ, LANES), 1)
    kaug_scr[:, HEAD_DIM:] = jnp.where(key_blk == key_lane, 1.0, 0.0).astype(BF16)
    km_scr[...] = jnp.zeros_like(km_scr)
    for jb in range(N_BLOCKS):
        kb = k_ref[jb * blk:(jb + 1) * blk, :].astype(F32)
        km_scr[jb:jb + 1, :] = jnp.mean(kb, axis=0, keepdims=True)
    km = km_scr[...].astype(BF16)

    q_all = q_ref[...]
    gate_t = lax.dot_general(km, q_all, NT_DIMS, preferred_element_type=F32)
    cand = lax.broadcasted_iota(I32, (SUBLANES, SEQ), 0)
    own = lax.broadcasted_iota(I32, (SUBLANES, SEQ), 1) >> 8
    past = cand < own
    rem = jnp.where(past, gate_t[0:N_BLOCKS, :], NEG_INF)
    keep = jnp.where(cand == own, 1.0, 0.0)
    for _ in range(MOBA_TOPK):
        m = jnp.max(rem, axis=0, keepdims=True)
        first = jnp.min(jnp.where(rem == m, cand, N_BLOCKS), axis=0, keepdims=True)
        hit = cand == first
        keep = jnp.where(jnp.logical_and(hit, past), 1.0, keep)
        rem = jnp.where(hit, -jnp.inf, rem)
    bias_t = jnp.concatenate([jnp.where(keep > 0.5, 0.0, NEG_INF),
                              jnp.zeros((LANES - SUBLANES, SEQ), F32)], axis=0)
    qaug_scr[:, :HEAD_DIM] = q_all
    qaug_scr[:, HEAD_DIM:] = bias_t.T.astype(BF16)

    for jb in range(N_BLOCKS):
        r0 = jb * blk
        s_scr[r0:, r0:r0 + blk] = lax.dot_general(
            qaug_scr[r0:, :], kaug_scr[r0:r0 + blk, :], NT_DIMS, preferred_element_type=F32)

    row = lax.broadcasted_iota(I32, (blk, blk), 0)
    col = lax.broadcasted_iota(I32, (blk, blk), 1)
    denom = []
    for qi in range(N_BLOCKS):
        q0 = qi * blk
        s_own = jnp.where(col <= row, s_scr[q0:q0 + blk, q0:q0 + blk], NEG_INF)
        m = jnp.max(s_own, axis=-1, keepdims=True)
        if qi > 0:
            s_past = s_scr[q0:q0 + blk, 0:q0]
            m = jnp.maximum(m, jnp.max(s_past, axis=-1, keepdims=True))
        p_own = jnp.exp(s_own - m)
        l = jnp.sum(p_own, axis=-1, keepdims=True)
        p_scr[q0:q0 + blk, q0:q0 + blk] = p_own.astype(BF16)
        if qi > 0:
            p_past = jnp.exp(s_past - m)
            l = l + jnp.sum(p_past, axis=-1, keepdims=True)
            p_scr[q0:q0 + blk, 0:q0] = p_past.astype(BF16)
        denom.append(l)

    for jb in range(N_BLOCKS):
        r0 = jb * blk
        pv = _dot(p_scr[r0:, r0:r0 + blk], v_ref[r0:r0 + blk, :])
        if jb == 0:
            oacc_scr[...] = pv
        else:
            oacc_scr[r0:, :] += pv
    for qi in range(N_BLOCKS):
        q0 = qi * blk
        o_ref[q0:q0 + blk, :] = (oacc_scr[q0:q0 + blk, :] / denom[qi]).astype(BF16)


def _attention(qkv):
    return pl.pallas_call(
        _attn_kernel,
        grid=(BATCH, N_HEADS),
        in_specs=[
            pl.BlockSpec((SEQ, HEAD_DIM), lambda b, h: (b, h)),
            pl.BlockSpec((SEQ, HEAD_DIM), lambda b, h: (b, N_HEADS + h)),
            pl.BlockSpec((SEQ, HEAD_DIM), lambda b, h: (b, 2 * N_HEADS + h)),
        ],
        out_specs=pl.BlockSpec((SEQ, HEAD_DIM), lambda b, h: (b, h)),
        out_shape=jax.ShapeDtypeStruct((N_TOK, D_MODEL), BF16),
        scratch_shapes=[pltpu.VMEM((SEQ, 2 * HEAD_DIM), BF16),
                        pltpu.VMEM((SEQ, 2 * HEAD_DIM), BF16),
                        pltpu.VMEM((SEQ, SEQ), F32),
                        pltpu.VMEM((SEQ, SEQ), BF16),
                        pltpu.VMEM((SEQ, HEAD_DIM), F32),
                        pltpu.VMEM((LANES, HEAD_DIM), F32)],
        compiler_params=_cparams(2),
        name="moba_attention",
    )(qkv, qkv, qkv)


OPROJ_TM = 512


def _oproj_kernel(a_ref, x_ref, mod_ref, ng_ref, w_ref, rw_ref, rb_ref,
                  x1_ref, h2_ref, idx_ref, wgt_ref, rank_ref, cnt_ref, cnt_scr):
    i = pl.program_id(0)

    @pl.when(i == 0)
    def _():
        cnt_scr[...] = jnp.zeros_like(cnt_scr)

    x1 = x_ref[...] + mod_ref[0, 2:3, :] * _dot(a_ref[...], w_ref[...])
    x1_ref[...] = x1
    _route(x1, i, ng_ref, mod_ref, rw_ref, rb_ref,
           h2_ref, idx_ref, wgt_ref, rank_ref, cnt_ref, cnt_scr)


def _oproj(attn, x2, mod, ng, w_o, rw, rb):
    tm = OPROJ_TM
    row = lambda i: (i, 0)
    fixed = lambda i: (0, 0)
    return pl.pallas_call(
        _oproj_kernel,
        grid=(N_TOK // tm,),
        in_specs=[
            pl.BlockSpec((tm, D_MODEL), row),
            pl.BlockSpec((tm, D_MODEL), row),
            pl.BlockSpec((1, N_ADA, D_MODEL), lambda i: (i * tm // SEQ, 0, 0)),
            pl.BlockSpec((2, D_MODEL), fixed),
            pl.BlockSpec((D_MODEL, D_MODEL), fixed, pipeline_mode=pl.Buffered(1)),
            pl.BlockSpec((D_MODEL, 2 * N_EXPERTS), fixed),
            pl.BlockSpec((1, N_EXPERTS), fixed),
        ],
        out_specs=(
            pl.BlockSpec((tm, D_MODEL), row),
            pl.BlockSpec((tm, D_MODEL), row),
            pl.BlockSpec((tm, LANES), row),
            pl.BlockSpec((tm, LANES), row),
            pl.BlockSpec((tm, LANES), row),
            pl.BlockSpec((1, N_EXPERTS), fixed),
        ),
        out_shape=_route_out_shapes(),
        scratch_shapes=[pltpu.VMEM((1, N_EXPERTS), F32)],
        compiler_params=_cparams(1),
        name="oproj",
    )(attn, x2, mod, ng, w_o, rw, rb)


def _split_hi_lo(w):
    hi = w.astype(BF16)
    lo = (w - hi.astype(F32)).astype(BF16)
    return jnp.concatenate([hi, lo], axis=1)


def _rope_tables(positions):
    inv = jnp.float32(ROPE_THETA) ** (-jnp.arange(0, ROT_DIM, 2, dtype=F32) / ROT_DIM)
    ang = positions.astype(F32).reshape(N_TOK, 1) * inv
    cos, sin = jnp.cos(ang), jnp.sin(ang)
    rest = HEAD_DIM - ROT_DIM
    cos_t = jnp.concatenate([cos, cos, jnp.ones((N_TOK, rest), F32)], axis=1)
    sin1_t = jnp.concatenate([-sin, jnp.zeros((N_TOK, HEAD_DIM - ROT_HALF), F32)], axis=1)
    sin2_t = jnp.concatenate([jnp.zeros((N_TOK, ROT_HALF), F32), sin,
                              jnp.zeros((N_TOK, rest), F32)], axis=1)
    return cos_t, sin1_t, sin2_t


def _experts(layer, h2, idx, rank, cnt, moe_w_gu, moe_b_gu, moe_w_down, moe_b_down):
    pos, plan = _plan(idx, rank, cnt)
    depth = moe_w_gu.shape[0]
    y_sorted = _moe(layer, h2, plan, moe_w_gu,
                    moe_b_gu.reshape(depth, N_EXPERTS, 1, 2 * D_FF), moe_w_down,
                    moe_b_down.reshape(depth, N_EXPERTS, 1, D_MODEL))
    return pos, y_sorted


def kernel(x, c, positions, norm_g, ada_w, ada_b, conv_w_in, conv_w, conv_w_out, kv_norm_g, w_kv,
           attn_w_q, attn_w_o, router_w, router_b, moe_w_gu, moe_b_gu, moe_w_down, moe_b_down,
           final_g):
    xf = x.reshape(N_TOK, D_MODEL)
    ada = _ada(c, ada_w, ada_b)
    mod = ada[:, :BATCH].reshape(2, BATCH, N_ADA, D_MODEL)
    cos_t, sin1_t, sin2_t = _rope_tables(positions)

    x1, h2, idx, wgt, rank, cnt = _mixer0(
        xf, mod[0], norm_g[0], conv_w_in[0].astype(BF16), conv_w[0], conv_w_out[0].astype(BF16),
        _split_hi_lo(router_w[0]), router_b[0].reshape(1, N_EXPERTS))
    pos, y_sorted = _experts(0, h2, idx, rank, cnt, moe_w_gu, moe_b_gu, moe_w_down, moe_b_down)

    w_qkv = jnp.concatenate([attn_w_q[0], w_kv], axis=1).astype(BF16)
    x2, qkv = _combine_qkv(pos, x1, mod[0], wgt, y_sorted, mod[1], norm_g[1], kv_norm_g, w_qkv,
                           cos_t, sin1_t, sin2_t)
    attn = _attention(qkv)
    x3, h2b, idx, wgt, rank, cnt = _oproj(
        attn, x2, mod[1], norm_g[1], attn_w_o[0].astype(BF16), _split_hi_lo(router_w[1]),
        router_b[1].reshape(1, N_EXPERTS))
    pos, y_sorted = _experts(1, h2b, idx, rank, cnt, moe_w_gu, moe_b_gu, moe_w_down, moe_b_down)
    out = _combine_final(pos, x3, mod[1], wgt, y_sorted, final_g)
    return out.reshape(BATCH, SEQ, D_MODEL)
```

```python
import jax
import jax.numpy as jnp
from jax import lax
from jax.experimental import pallas as pl
from jax.experimental.pallas import tpu as pltpu

F32 = jnp.float32
BF16 = jnp.bfloat16
I32 = jnp.int32

D_MODEL = 2048
BATCH = 4
SEQ = 2048
N_TOK = BATCH * SEQ
N_HEADS = 16
HEAD_DIM = 128
ROT_DIM = 32
ROT_HALF = ROT_DIM // 2
ROPE_THETA = 500000.0
MOBA_BLOCK = 256
MOBA_TOPK = 3
N_BLOCKS = SEQ // MOBA_BLOCK
N_EXPERTS = 32
TOP_K = 4
D_FF = D_MODEL
SWIGLU_ALPHA = 1.702
SWIGLU_LIMIT = 7.0
NORM_EPS = 1e-5
NEG_INF = -1e30
N_ADA = 6

LANES = 128
SUBLANES = 8
VMEM_LIMIT = 56 * 1024 * 1024
MOE_VMEM_LIMIT = 60 * 1024 * 1024

SUB = 128
GROUP_SUBS = 12
MAIN_SUBS = 4
PREFETCH_BATCH = 2 * SUB
N_SLOTS = N_TOK * TOP_K
N_ROWS = N_SLOTS + N_EXPERTS * SUB
N_GROUPS = N_EXPERTS + N_SLOTS // (SUB * GROUP_SUBS)
FF_TILE = 512
N_FF_TILES = D_FF // FF_TILE

NT_DIMS = (((1,), (1,)), ((), ()))


def _cparams(n_axes, vmem_limit=VMEM_LIMIT):
    return pltpu.CompilerParams(dimension_semantics=("arbitrary",) * n_axes,
                                vmem_limit_bytes=vmem_limit)


def _rms_scale(x):
    return x * lax.rsqrt(jnp.mean(x * x, axis=-1, keepdims=True) + NORM_EPS)


def _modulate(x, g, shift, scale):
    return (_rms_scale(x) * g) * (1 + scale) + shift


def _dot(a, b):
    return jnp.dot(a, b, preferred_element_type=F32)


ADA_TN = 1024


def _ada_kernel(c_ref, w_ref, b_ref, o_ref):
    c = c_ref[...]
    c_act = c * jax.nn.sigmoid(c)
    o_ref[0] = _dot(c_act.astype(BF16), w_ref[0].astype(BF16)) + b_ref[0]


def _ada(c, ada_w, ada_b):
    depth = ada_w.shape[0]
    n_out = ada_w.shape[2]
    c_pad = jnp.zeros((8, D_MODEL), F32).at[:BATCH].set(c)
    return pl.pallas_call(
        _ada_kernel,
        grid=(depth, n_out // ADA_TN),
        in_specs=[
            pl.BlockSpec((8, D_MODEL), lambda l, j: (0, 0)),
            pl.BlockSpec((1, D_MODEL, ADA_TN), lambda l, j: (l, 0, j)),
            pl.BlockSpec((1, 1, ADA_TN), lambda l, j: (l, 0, j)),
        ],
        out_specs=pl.BlockSpec((1, 8, ADA_TN), lambda l, j: (l, 0, j)),
        out_shape=jax.ShapeDtypeStruct((depth, 8, n_out), F32),
        compiler_params=_cparams(2),
        name="ada",
    )(c_pad, ada_w, ada_b.reshape(depth, 1, n_out))


def _route(x1, i, ng_ref, mod_ref, rw_ref, rb_ref,
           h2_ref, idx_ref, wgt_ref, rank_ref, cnt_ref, cnt_scr):
    tm = x1.shape[0]
    h2 = _modulate(x1, ng_ref[1:2, :], mod_ref[0, 3:4, :], mod_ref[0, 4:5, :])
    h2_ref[...] = h2
    h_hi = h2.astype(BF16)
    h_lo = (h2 - h_hi.astype(F32)).astype(BF16)
    r_hi = _dot(h_hi, rw_ref[...])
    r_lo = _dot(h_lo, rw_ref[...])
    logits = (r_hi[:, :N_EXPERTS] + r_hi[:, N_EXPERTS:] + r_lo[:, :N_EXPERTS]) + rb_ref[...]
    lane = lax.broadcasted_iota(I32, (tm, N_EXPERTS), 1)
    vals, idxs = [], []
    rem = logits
    for _ in range(TOP_K):
        m = jnp.max(rem, axis=-1, keepdims=True)
        ik = jnp.min(jnp.where(rem == m, lane, N_EXPERTS), axis=-1, keepdims=True)
        vals.append(m)
        idxs.append(ik)
        rem = jnp.where(lane == ik, -jnp.inf, rem)
    exps = [jnp.exp(v - vals[0]) for v in vals]
    den = exps[0] + exps[1] + exps[2] + exps[3]
    hot = jnp.zeros((tm, N_EXPERTS), F32)
    for ik in idxs:
        hot = jnp.where(lane == ik, 1.0, hot)
    row = lax.broadcasted_iota(I32, (tm, tm), 0)
    col = lax.broadcasted_iota(I32, (tm, tm), 1)
    tri = jnp.where(col < row, 1.0, 0.0).astype(BF16)
    prior = cnt_scr[...]
    rank_mat = _dot(tri, hot.astype(BF16)) + prior
    cnt_new = prior + jnp.sum(hot, axis=0, keepdims=True)
    cnt_scr[...] = cnt_new
    cnt_ref[...] = cnt_new
    lane_o = lax.broadcasted_iota(I32, (tm, LANES), 1)
    idx_o = jnp.zeros((tm, LANES), I32)
    wgt_o = jnp.zeros((tm, LANES), F32)
    rank_o = jnp.zeros((tm, LANES), I32)
    for k in range(TOP_K):
        rk = jnp.sum(jnp.where(lane == idxs[k], rank_mat, 0.0), axis=-1, keepdims=True)
        idx_o = jnp.where(lane_o == k, idxs[k], idx_o)
        wgt_o = jnp.where(lane_o == k, exps[k] / den, wgt_o)
        rank_o = jnp.where(lane_o == k, rk.astype(I32), rank_o)
    idx_ref[...] = idx_o
    wgt_ref[...] = wgt_o
    rank_ref[...] = rank_o


def _route_out_shapes():
    return (
        jax.ShapeDtypeStruct((N_TOK, D_MODEL), F32),
        jax.ShapeDtypeStruct((N_TOK, D_MODEL), F32),
        jax.ShapeDtypeStruct((N_TOK, LANES), I32),
        jax.ShapeDtypeStruct((N_TOK, LANES), F32),
        jax.ShapeDtypeStruct((N_TOK, LANES), I32),
        jax.ShapeDtypeStruct((1, N_EXPERTS), F32),
    )


MIX_TM = 512
MIX_TN = 512
MIX_NC = D_MODEL // MIX_TN


def _mixer0_kernel(x_ref, mod_ref, ng_ref, wb_ref, wc_ref, wu_ref, cw_ref, wo_ref, rw_ref, rb_ref,
                   x1_ref, h2_ref, idx_ref, wgt_ref, rank_ref, cnt_ref,
                   h_scr, vpad_scr, carry_scr, cnt_scr):
    i = pl.program_id(0)
    j = pl.program_id(1)
    tm = MIX_TM

    @pl.when(jnp.logical_and(i == 0, j == 0))
    def _():
        cnt_scr[...] = jnp.zeros_like(cnt_scr)
        carry_scr[...] = jnp.zeros_like(carry_scr)

    @pl.when(j == 0)
    def _():
        h = _modulate(x_ref[...], ng_ref[0:1, :], mod_ref[0, 0:1, :], mod_ref[0, 1:2, :])
        h_scr[...] = h.astype(BF16)
        x1_ref[...] = jnp.zeros_like(x1_ref)

    h = h_scr[...]
    b_gate = _dot(h, wb_ref[...])
    v = _dot(h, wc_ref[...]) * _dot(h, wu_ref[...])
    seq_start = (i % (SEQ // tm)) == 0
    vpad_scr[0:8, :] = jnp.where(seq_start, 0.0, carry_scr[j])
    vpad_scr[8:8 + tm, :] = v
    carry_scr[j] = v[tm - 8:tm, :]
    cw = cw_ref[...]
    conv = (cw[0:1, :] * vpad_scr[6:6 + tm, :] + cw[1:2, :] * vpad_scr[7:7 + tm, :]
            + cw[2:3, :] * v)
    x1_ref[...] += _dot((b_gate * conv).astype(BF16), wo_ref[...])

    @pl.when(j == MIX_NC - 1)
    def _():
        x1 = x_ref[...] + mod_ref[0, 2:3, :] * x1_ref[...]
        x1_ref[...] = x1
        _route(x1, i, ng_ref, mod_ref, rw_ref, rb_ref,
               h2_ref, idx_ref, wgt_ref, rank_ref, cnt_ref, cnt_scr)


def _mixer0(x, mod, ng, w_in, cw, w_out, rw, rb):
    tm, tn, nc = MIX_TM, MIX_TN, MIX_NC
    row = lambda i, j: (i, 0)
    fixed = lambda i, j: (0, 0)
    return pl.pallas_call(
        _mixer0_kernel,
        grid=(N_TOK // tm, nc),
        in_specs=[
            pl.BlockSpec((tm, D_MODEL), row),
            pl.BlockSpec((1, N_ADA, D_MODEL), lambda i, j: (i * tm // SEQ, 0, 0)),
            pl.BlockSpec((2, D_MODEL), fixed),
            pl.BlockSpec((D_MODEL, tn), lambda i, j: (0, j)),
            pl.BlockSpec((D_MODEL, tn), lambda i, j: (0, nc + j)),
            pl.BlockSpec((D_MODEL, tn), lambda i, j: (0, 2 * nc + j)),
            pl.BlockSpec((3, tn), lambda i, j: (0, j)),
            pl.BlockSpec((tn, D_MODEL), lambda i, j: (j, 0)),
            pl.BlockSpec((D_MODEL, 2 * N_EXPERTS), fixed),
            pl.BlockSpec((1, N_EXPERTS), fixed),
        ],
        out_specs=(
            pl.BlockSpec((tm, D_MODEL), row),
            pl.BlockSpec((tm, D_MODEL), row),
            pl.BlockSpec((tm, LANES), row),
            pl.BlockSpec((tm, LANES), row),
            pl.BlockSpec((tm, LANES), row),
            pl.BlockSpec((1, N_EXPERTS), fixed),
        ),
        out_shape=_route_out_shapes(),
        scratch_shapes=[
            pltpu.VMEM((tm, D_MODEL), BF16),
            pltpu.VMEM((tm + 8, tn), F32),
            pltpu.VMEM((nc, 8, tn), F32),
            pltpu.VMEM((1, N_EXPERTS), F32),
        ],
        compiler_params=_cparams(2),
        name="mixer0",
    )(x, mod, ng, w_in, w_in, w_in, cw, w_out, rw, rb)


def _moe_kernel(ge_ref, gr_ref, gn_ref, tok_ref, tail_ref,
                h_hbm, wg_ref, wu_ref, wd_ref, bg_ref, bu_ref, bd_ref,
                out_hbm,
                xg, stage, acc, pf_ref, gsem, osem):
    g = pl.program_id(0)
    j = pl.program_id(1)
    ns = gn_ref[g]
    row0 = gr_ref[g]
    n_rows = ns * SUB
    g_next = jnp.minimum(g + 1, N_GROUPS - 1)
    next_row0 = gr_ref[g_next]
    next_rows = jnp.where(g + 1 < N_GROUPS, gn_ref[g_next], 0) * SUB

    def row_copy(tok, c, u):
        return pltpu.make_async_copy(h_hbm.at[pl.ds(tok, 1), :],
                                     stage.at[c, pl.ds(u, 1), :], gsem)

    def sub_rows(s):
        return pl.ds(pl.multiple_of(s * SUB, SUB), SUB)

    @pl.when(jnp.logical_and(g == 0, j == 0))
    def _():
        pf_ref[0] = 0

    @pl.when(jnp.logical_and(j == 0, ns > 0))
    def _():
        covered = jnp.minimum(pf_ref[0], n_rows)

        def fetch(c, carry):
            for u in range(SUBLANES):
                row_copy(tok_ref[row0 + c * SUBLANES + u], c, u).start()
            return carry

        lax.fori_loop(covered // SUBLANES, n_rows // SUBLANES, fetch, 0)

        def wait(c, carry):
            for u in range(SUBLANES):
                row_copy(0, c, u).wait()
            return carry

        lax.fori_loop(0, n_rows // SUBLANES, wait, 0)
        bias = jnp.broadcast_to(bd_ref[0], (SUB, D_MODEL))

        def cast(s, carry):
            tiles = pl.ds(pl.multiple_of(s * (SUB // SUBLANES), SUB // SUBLANES), SUB // SUBLANES)
            xg[sub_rows(s), :] = stage[tiles].reshape(SUB, D_MODEL).astype(BF16)
            acc[sub_rows(s), :] = bias
            return carry

        lax.fori_loop(0, ns, cast, 0)
        pf_ref[0] = 0

    def prefetch(n_offer):
        ptr = pf_ref[0]
        c0 = ptr // SUBLANES
        for u in range(n_offer):
            tok = tok_ref[next_row0 + ptr + u]
            row_copy(tok, c0 + u // SUBLANES, u % SUBLANES).start()
        pf_ref[0] = ptr + n_offer

    def out_copy(s):
        dst = pl.multiple_of(row0 + s * SUB, SUB)
        return pltpu.make_async_copy(acc.at[sub_rows(s), :],
                                     out_hbm.at[pl.ds(dst, SUB), :], osem)

    last_tile = j == N_FF_TILES - 1

    def chunk(s0, n_sub, n_prefetch):
        rows = pl.ds(pl.multiple_of(s0 * SUB, SUB), n_sub * SUB)
        xs = xg[rows, :]
        gate = _dot(xs, wg_ref[0].astype(BF16)) + bg_ref[0]
        up = _dot(xs, wu_ref[0].astype(BF16)) + bu_ref[0]
        gate = jnp.minimum(gate, SWIGLU_LIMIT)
        up = jnp.clip(up, -SWIGLU_LIMIT, SWIGLU_LIMIT)
        glu = gate * jax.nn.sigmoid(gate * SWIGLU_ALPHA)
        acc[rows, :] += _dot(((up + 1) * glu).astype(BF16), wd_ref[0].astype(BF16))
        if n_prefetch:
            prefetch(n_prefetch)
        for t in range(n_sub):
            @pl.when(last_tile)
            def _():
                out_copy(s0 + t).start()

    @pl.when(ns > 0)
    def _():
        def main(c, carry):
            left = next_rows - pf_ref[0]
            s0 = c * MAIN_SUBS
            lax.cond(left >= PREFETCH_BATCH,
                     lambda: chunk(s0, MAIN_SUBS, PREFETCH_BATCH),
                     lambda: lax.cond(left >= SUB,
                                      lambda: chunk(s0, MAIN_SUBS, SUB),
                                      lambda: chunk(s0, MAIN_SUBS, 0)))
            return carry

        lax.fori_loop(0, ns // MAIN_SUBS, main, 0)
        for rem in range(1, MAIN_SUBS):
            @pl.when(ns % MAIN_SUBS == rem)
            def _():
                chunk(ns - rem, rem, 0)

        @pl.when(last_tile)
        def _():
            def wait(s, carry):
                out_copy(s).wait()
                return carry

            lax.fori_loop(0, ns, wait, 0)

    @pl.when(jnp.logical_and(g == N_GROUPS - 1, j == N_FF_TILES - 1))
    def _():
        acc[0:SUB, :] = jnp.zeros((SUB, D_MODEL), F32)
        tail0 = tail_ref[0]
        n_tail = (N_ROWS - tail0) // SUB

        def tail_copy(s):
            dst = pl.multiple_of(tail0 + s * SUB, SUB)
            return pltpu.make_async_copy(acc.at[0:SUB, :], out_hbm.at[pl.ds(dst, SUB), :], osem)

        def start(s, carry):
            tail_copy(s).start()
            return carry

        def wait(s, carry):
            tail_copy(s).wait()
            return carry

        lax.fori_loop(0, n_tail, start, 0)
        lax.fori_loop(0, n_tail, wait, 0)


def _moe(layer, h2, plan, w_gu, b_gu, w_down, b_down):
    ge, gr, gn, tok, tail = plan
    nj = N_FF_TILES

    def jj(g, j, gn_ref):
        return jnp.where(gn_ref[g] > 0, j, nj - 1)

    sq = pl.Squeezed()
    grid_spec = pltpu.PrefetchScalarGridSpec(
        num_scalar_prefetch=5,
        grid=(N_GROUPS, nj),
        in_specs=[
            pl.BlockSpec(memory_space=pl.ANY),
            pl.BlockSpec((sq, 1, D_MODEL, FF_TILE),
                         lambda g, j, ge, gr, gn, *_: (layer, ge[g], 0, jj(g, j, gn))),
            pl.BlockSpec((sq, 1, D_MODEL, FF_TILE),
                         lambda g, j, ge, gr, gn, *_: (layer, ge[g], 0, nj + jj(g, j, gn))),
            pl.BlockSpec((sq, 1, FF_TILE, D_MODEL),
                         lambda g, j, ge, gr, gn, *_: (layer, ge[g], jj(g, j, gn), 0)),
            pl.BlockSpec((sq, 1, 1, FF_TILE),
                         lambda g, j, ge, gr, gn, *_: (layer, ge[g], 0, jj(g, j, gn))),
            pl.BlockSpec((sq, 1, 1, FF_TILE),
                         lambda g, j, ge, gr, gn, *_: (layer, ge[g], 0, nj + jj(g, j, gn))),
            pl.BlockSpec((sq, 1, 1, D_MODEL),
                         lambda g, j, ge, gr, gn, *_: (layer, ge[g], 0, 0)),
        ],
        out_specs=pl.BlockSpec(memory_space=pl.ANY),
        scratch_shapes=[
            pltpu.VMEM((GROUP_SUBS * SUB, D_MODEL), BF16),
            pltpu.VMEM((GROUP_SUBS * SUB // SUBLANES, SUBLANES, D_MODEL), F32),
            pltpu.VMEM((GROUP_SUBS * SUB, D_MODEL), F32),
            pltpu.SMEM((1,), I32),
            pltpu.SemaphoreType.DMA(()),
            pltpu.SemaphoreType.DMA(()),
        ],
    )
    return pl.pallas_call(
        _moe_kernel,
        grid_spec=grid_spec,
        out_shape=jax.ShapeDtypeStruct((N_ROWS, D_MODEL), F32),
        compiler_params=_cparams(2, MOE_VMEM_LIMIT),
        name="moe",
    )(ge, gr, gn, tok, tail, h2, w_gu, w_gu, w_down, b_gu, b_gu, b_down)


def _plan(idx, rank, cnt):
    idx = idx[:, :TOP_K]
    rank = rank[:, :TOP_K]
    counts = cnt.reshape(N_EXPERTS).astype(I32)
    ntile = (counts + SUB - 1) // SUB
    start = (jnp.cumsum(ntile) - ntile) * SUB
    onehot = idx[..., None] == jnp.arange(N_EXPERTS, dtype=I32)
    pos = jnp.sum(jnp.where(onehot, start, 0), axis=-1) + rank
    pos = pos.reshape(N_SLOTS).astype(I32)
    tok = jnp.zeros((N_ROWS,), I32).at[pos].set(
        jnp.repeat(jnp.arange(N_TOK, dtype=I32), TOP_K), unique_indices=True)
    ngrp = (ntile + GROUP_SUBS - 1) // GROUP_SUBS
    gcum = jnp.cumsum(ngrp)
    total = gcum[-1]
    g = jnp.arange(N_GROUPS, dtype=I32)
    e_g = jnp.sum((gcum[None, :] <= g[:, None]).astype(I32), axis=1)
    e_g = jnp.minimum(e_g, N_EXPERTS - 1)
    local = g - (gcum - ngrp)[e_g]
    valid = g < total
    nsub = jnp.where(valid, jnp.minimum(GROUP_SUBS, ntile[e_g] - local * GROUP_SUBS), 0)
    row0 = jnp.where(valid, start[e_g] + local * GROUP_SUBS * SUB, 0)
    e_last = e_g[jnp.maximum(total - 1, 0)]
    e_g = jnp.where(valid, e_g, e_last)
    tail = (jnp.sum(ntile) * SUB).astype(I32).reshape(1)
    return pos, (e_g.astype(I32), row0.astype(I32), nsub.astype(I32), tok, tail)


COMB_TM = 256


def _combine_final_kernel(pos_ref, x1_ref, mod_ref, wgt_ref, y_hbm, fg_ref, o_ref, buf, sem):
    i = pl.program_id(0)
    n = pl.num_programs(0)
    tm = COMB_TM

    def row_copy(p, slot, k, c, u):
        return pltpu.make_async_copy(y_hbm.at[pl.ds(p, 1), :],
                                     buf.at[slot, k, c, pl.ds(u, 1), :], sem.at[slot])

    def issue(tile, slot):
        base = tile * (tm * TOP_K)

        def body(c, carry):
            for u in range(SUBLANES):
                for k in range(TOP_K):
                    p = pos_ref[base + (c * SUBLANES + u) * TOP_K + k]
                    row_copy(p, slot, k, c, u).start()
            return carry

        lax.fori_loop(0, tm // SUBLANES, body, 0)

    def wait(slot):
        def body(c, carry):
            for u in range(SUBLANES):
                for k in range(TOP_K):
                    row_copy(0, slot, k, c, u).wait()
            return carry

        lax.fori_loop(0, tm // SUBLANES, body, 0)

    @pl.when(i == 0)
    def _():
        issue(0, 0)

    @pl.when(i + 1 < n)
    def _():
        issue(i + 1, (i + 1) % 2)

    slot = i % 2
    wait(slot)
    w = wgt_ref[...]
    y = w[:, 0:1] * buf[slot, 0].reshape(tm, D_MODEL)
    for k in range(1, TOP_K):
        y = y + w[:, k:k + 1] * buf[slot, k].reshape(tm, D_MODEL)
    x2 = x1_ref[...] + mod_ref[0, 5:6, :] * y
    o_ref[...] = _rms_scale(x2) * fg_ref[...]


def _combine_final(pos, x1, mod, wgt, y_sorted, final_g):
    tm = COMB_TM
    grid_spec = pltpu.PrefetchScalarGridSpec(
        num_scalar_prefetch=1,
        grid=(N_TOK // tm,),
        in_specs=[
            pl.BlockSpec((tm, D_MODEL), lambda i, pos: (i, 0)),
            pl.BlockSpec((1, N_ADA, D_MODEL), lambda i, pos: (i * tm // SEQ, 0, 0)),
            pl.BlockSpec((tm, LANES), lambda i, pos: (i, 0)),
            pl.BlockSpec(memory_space=pl.ANY),
            pl.BlockSpec((1, D_MODEL), lambda i, pos: (0, 0)),
        ],
        out_specs=pl.BlockSpec((tm, D_MODEL), lambda i, pos: (i, 0)),
        scratch_shapes=[
            pltpu.VMEM((2, TOP_K, tm // SUBLANES, SUBLANES, D_MODEL), F32),
            pltpu.SemaphoreType.DMA((2,)),
        ],
    )
    return pl.pallas_call(
        _combine_final_kernel,
        grid_spec=grid_spec,
        out_shape=jax.ShapeDtypeStruct((N_TOK, D_MODEL), F32),
        compiler_params=_cparams(1),
        name="combine_final",
    )(pos, x1, mod, wgt, y_sorted, final_g.reshape(1, D_MODEL))


QKV_TM = 512
QKV_TN = 1024
QKV_NC = D_MODEL // QKV_TN


QKV_GATHER_ROWS = 128
QKV_GATHER_STEPS = QKV_TM // QKV_GATHER_ROWS


def _combine_qkv_kernel(pos_ref, x1_ref, mod0_ref, wgt_ref, y_hbm,
                        mod_ref, ng_ref, kvg_ref, w_ref, c_ref, s1_ref, s2_ref,
                        x2_ref, o_ref, h_scr, stage, sem):
    i = pl.program_id(0)
    j = pl.program_id(1)
    tm = QKV_TM
    which = j // QKV_NC

    def row_copy(p, k, c, u):
        return pltpu.make_async_copy(y_hbm.at[pl.ds(p, 1), :],
                                     stage.at[k, c, pl.ds(u, 1), :], sem)

    @pl.when(jnp.logical_and(i == 0, j == 0))
    def _():
        def fetch(c, carry):
            for u in range(SUBLANES):
                for k in range(TOP_K):
                    row_copy(pos_ref[(c * SUBLANES + u) * TOP_K + k], k, c, u).start()
            return carry

        lax.fori_loop(0, tm // SUBLANES, fetch, 0)

    @pl.when(j == 0)
    def _():
        def wait(c, carry):
            for u in range(SUBLANES):
                for k in range(TOP_K):
                    row_copy(0, k, c, u).wait()
            return carry

        lax.fori_loop(0, tm // SUBLANES, wait, 0)
        w = wgt_ref[...]
        y = w[:, 0:1] * stage[0].reshape(tm, D_MODEL)
        for k in range(1, TOP_K):
            y = y + w[:, k:k + 1] * stage[k].reshape(tm, D_MODEL)
        x2 = x1_ref[...] + mod0_ref[0, 5:6, :] * y
        x2_ref[...] = x2
        xn = _rms_scale(x2)
        hq = (xn * ng_ref[0:1, :]) * (1 + mod_ref[0, 1:2, :]) + mod_ref[0, 0:1, :]
        h_scr[0] = hq.astype(BF16)
        h_scr[1] = (xn * kvg_ref[...]).astype(BF16)

    def project():
        z = _dot(h_scr[jnp.minimum(which, 1)], w_ref[...])
        is_rot = which < 2
        scale = jnp.where(which == 0, HEAD_DIM ** -0.5, 1.0).astype(F32)
        cos = jnp.where(is_rot, c_ref[...], 1.0)
        s1 = jnp.where(is_rot, s1_ref[...], 0.0)
        s2 = jnp.where(is_rot, s2_ref[...], 0.0)
        for hh in range(QKV_TN // HEAD_DIM):
            sl = slice(hh * HEAD_DIM, (hh + 1) * HEAD_DIM)
            zz = z[:, sl]
            rot = (zz * cos + pltpu.roll(zz, HEAD_DIM - ROT_HALF, 1) * s1
                   + pltpu.roll(zz, ROT_HALF, 1) * s2)
            o_ref[:, sl] = (rot * scale).astype(BF16)

    def project_and_gather():
        base = ((i + 1) * tm + j * QKV_GATHER_ROWS) * TOP_K
        c0 = j * (QKV_GATHER_ROWS // SUBLANES)
        for u in range(QKV_GATHER_ROWS):
            for k in range(TOP_K):
                row_copy(pos_ref[base + u * TOP_K + k], k, c0 + u // SUBLANES,
                         u % SUBLANES).start()
        project()

    has_batch = jnp.logical_and(i + 1 < pl.num_programs(0), j < QKV_GATHER_STEPS)
    lax.cond(has_batch, project_and_gather, project)


def _combine_qkv(pos, x1, mod0, wgt, y_sorted, mod, ng, kvg, w_qkv, cos_t, sin1_t, sin2_t):
    tm, tn = QKV_TM, QKV_TN
    row = lambda i, j, pos: (i, 0)
    fixed = lambda i, j, pos: (0, 0)
    batch = lambda i, j, pos: (i * tm // SEQ, 0, 0)
    grid_spec = pltpu.PrefetchScalarGridSpec(
        num_scalar_prefetch=1,
        grid=(N_TOK // tm, 3 * QKV_NC),
        in_specs=[
            pl.BlockSpec((tm, D_MODEL), row),
            pl.BlockSpec((1, N_ADA, D_MODEL), batch),
            pl.BlockSpec((tm, LANES), row),
            pl.BlockSpec(memory_space=pl.ANY),
            pl.BlockSpec((1, N_ADA, D_MODEL), batch),
            pl.BlockSpec((2, D_MODEL), fixed),
            pl.BlockSpec((1, D_MODEL), fixed),
            pl.BlockSpec((D_MODEL, tn), lambda i, j, pos: (0, j)),
            pl.BlockSpec((tm, HEAD_DIM), row),
            pl.BlockSpec((tm, HEAD_DIM), row),
            pl.BlockSpec((tm, HEAD_DIM), row),
        ],
        out_specs=(
            pl.BlockSpec((tm, D_MODEL), row),
            pl.BlockSpec((tm, tn), lambda i, j, pos: (i, j)),
        ),
        scratch_shapes=[
            pltpu.VMEM((2, tm, D_MODEL), BF16),
            pltpu.VMEM((TOP_K, tm // SUBLANES, SUBLANES, D_MODEL), F32),
            pltpu.SemaphoreType.DMA(()),
        ],
    )
    return pl.pallas_call(
        _combine_qkv_kernel,
        grid_spec=grid_spec,
        out_shape=(jax.ShapeDtypeStruct((N_TOK, D_MODEL), F32),
                   jax.ShapeDtypeStruct((N_TOK, 3 * D_MODEL), BF16)),
        compiler_params=_cparams(2),
        name="combine_qkv",
    )(pos, x1, mod0, wgt, y_sorted, mod, ng, kvg.reshape(1, D_MODEL), w_qkv, cos_t, sin1_t, sin2_t)


def _attn_kernel(q_ref, k_ref, v_ref, o_ref, qaug_scr, kaug_scr, s_scr, p_scr, oacc_scr, km_scr):
    blk = MOBA_BLOCK
    kaug_scr[:, :HEAD_DIM] = k_ref[...]
    key_blk = lax.broadcasted_iota(I32, (SEQ, LANES), 0) >> 8
    key_lane = lax.broadcasted_iota(I32, (SEQ, LANES), 1)
    kaug_scr[:, HEAD_DIM:] = jnp.where(key_blk == key_lane, 1.0, 0.0).astype(BF16)
    km_scr[...] = jnp.zeros_like(km_scr)
    for jb in range(N_BLOCKS):
        kb = k_ref[jb * blk:(jb + 1) * blk, :].astype(F32)
        km_scr[jb:jb + 1, :] = jnp.mean(kb, axis=0, keepdims=True)
    km = km_scr[...].astype(BF16)

    q_all = q_ref[...]
    gate_t = lax.dot_general(km, q_all, NT_DIMS, preferred_element_type=F32)
    cand = lax.broadcasted_iota(I32, (SUBLANES, SEQ), 0)
    own = lax.broadcasted_iota(I32, (SUBLANES, SEQ), 1) >> 8
    past = cand < own
    rem = jnp.where(past, gate_t[0:N_BLOCKS, :], NEG_INF)
    keep = jnp.where(cand == own, 1.0, 0.0)
    for _ in range(MOBA_TOPK):
        m = jnp.max(rem, axis=0, keepdims=True)
        first = jnp.min(jnp.where(rem == m, cand, N_BLOCKS), axis=0, keepdims=True)
        hit = cand == first
        keep = jnp.where(jnp.logical_and(hit, past), 1.0, keep)
        rem = jnp.where(hit, -jnp.inf, rem)
    bias_t = jnp.concatenate([jnp.where(keep > 0.5, 0.0, NEG_INF),
                              jnp.zeros((LANES - SUBLANES, SEQ), F32)], axis=0)
    qaug_scr[:, :HEAD_DIM] = q_all
    qaug_scr[:, HEAD_DIM:] = bias_t.T.astype(BF16)

    for jb in range(N_BLOCKS):
        r0 = jb * blk
        s_scr[r0:, r0:r0 + blk] = lax.dot_general(
            qaug_scr[r0:, :], kaug_scr[r0:r0 + blk, :], NT_DIMS, preferred_element_type=F32)

    row = lax.broadcasted_iota(I32, (blk, blk), 0)
    col = lax.broadcasted_iota(I32, (blk, blk), 1)
    denom = []
    for qi in range(N_BLOCKS):
        q0 = qi * blk
        s_own = jnp.where(col <= row, s_scr[q0:q0 + blk, q0:q0 + blk], NEG_INF)
        m = jnp.max(s_own, axis=-1, keepdims=True)
        if qi > 0:
            s_past = s_scr[q0:q0 + blk, 0:q0]
            m = jnp.maximum(m, jnp.max(s_past, axis=-1, keepdims=True))
        p_own = jnp.exp(s_own - m)
        l = jnp.sum(p_own, axis=-1, keepdims=True)
        p_scr[q0:q0 + blk, q0:q0 + blk] = p_own.astype(BF16)
        if qi > 0:
            p_past = jnp.exp(s_past - m)
            l = l + jnp.sum(p_past, axis=-1, keepdims=True)
            p_scr[q0:q0 + blk, 0:q0] = p_past.astype(BF16)
        denom.append(l)

    for jb in range(N_BLOCKS):
        r0 = jb * blk
        pv = _dot(p_scr[r0:, r0:r0 + blk], v_ref[r0:r0 + blk, :])
        if jb == 0:
            oacc_scr[...] = pv
        else:
            oacc_scr[r0:, :] += pv
    for qi in range(N_BLOCKS):
        q0 = qi * blk
        o_ref[q0:q0 + blk, :] = (oacc_scr[q0:q0 + blk, :] / denom[qi]).astype(BF16)


def _attention(qkv):
    return pl.pallas_call(
        _attn_kernel,
        grid=(BATCH, N_HEADS),
        in_specs=[
            pl.BlockSpec((SEQ, HEAD_DIM), lambda b, h: (b, h)),
            pl.BlockSpec((SEQ, HEAD_DIM), lambda b, h: (b, N_HEADS + h)),
            pl.BlockSpec((SEQ, HEAD_DIM), lambda b, h: (b, 2 * N_HEADS + h)),
        ],
        out_specs=pl.BlockSpec((SEQ, HEAD_DIM), lambda b, h: (b, h)),
        out_shape=jax.ShapeDtypeStruct((N_TOK, D_MODEL), BF16),
        scratch_shapes=[pltpu.VMEM((SEQ, 2 * HEAD_DIM), BF16),
                        pltpu.VMEM((SEQ, 2 * HEAD_DIM), BF16),
                        pltpu.VMEM((SEQ, SEQ), F32),
                        pltpu.VMEM((SEQ, SEQ), BF16),
                        pltpu.VMEM((SEQ, HEAD_DIM), F32),
                        pltpu.VMEM((LANES, HEAD_DIM), F32)],
        compiler_params=_cparams(2),
        name="moba_attention",
    )(qkv, qkv, qkv)


OPROJ_TM = 512


def _oproj_kernel(a_ref, x_ref, mod_ref, ng_ref, w_ref, rw_ref, rb_ref,
                  x1_ref, h2_ref, idx_ref, wgt_ref, rank_ref, cnt_ref, cnt_scr):
    i = pl.program_id(0)

    @pl.when(i == 0)
    def _():
        cnt_scr[...] = jnp.zeros_like(cnt_scr)

    x1 = x_ref[...] + mod_ref[0, 2:3, :] * _dot(a_ref[...], w_ref[...])
    x1_ref[...] = x1
    _route(x1, i, ng_ref, mod_ref, rw_ref, rb_ref,
           h2_ref, idx_ref, wgt_ref, rank_ref, cnt_ref, cnt_scr)


def _oproj(attn, x2, mod, ng, w_o, rw, rb):
    tm = OPROJ_TM
    row = lambda i: (i, 0)
    fixed = lambda i: (0, 0)
    return pl.pallas_call(
        _oproj_kernel,
        grid=(N_TOK // tm,),
        in_specs=[
            pl.BlockSpec((tm, D_MODEL), row),
            pl.BlockSpec((tm, D_MODEL), row),
            pl.BlockSpec((1, N_ADA, D_MODEL), lambda i: (i * tm // SEQ, 0, 0)),
            pl.BlockSpec((2, D_MODEL), fixed),
            pl.BlockSpec((D_MODEL, D_MODEL), fixed, pipeline_mode=pl.Buffered(1)),
            pl.BlockSpec((D_MODEL, 2 * N_EXPERTS), fixed),
            pl.BlockSpec((1, N_EXPERTS), fixed),
        ],
        out_specs=(
            pl.BlockSpec((tm, D_MODEL), row),
            pl.BlockSpec((tm, D_MODEL), row),
            pl.BlockSpec((tm, LANES), row),
            pl.BlockSpec((tm, LANES), row),
            pl.BlockSpec((tm, LANES), row),
            pl.BlockSpec((1, N_EXPERTS), fixed),
        ),
        out_shape=_route_out_shapes(),
        scratch_shapes=[pltpu.VMEM((1, N_EXPERTS), F32)],
        compiler_params=_cparams(1),
        name="oproj",
    )(attn, x2, mod, ng, w_o, rw, rb)


def _split_hi_lo(w):
    hi = w.astype(BF16)
    lo = (w - hi.astype(F32)).astype(BF16)
    return jnp.concatenate([hi, lo], axis=1)


def _rope_tables(positions):
    inv = jnp.float32(ROPE_THETA) ** (-jnp.arange(0, ROT_DIM, 2, dtype=F32) / ROT_DIM)
    ang = positions.astype(F32).reshape(N_TOK, 1) * inv
    cos, sin = jnp.cos(ang), jnp.sin(ang)
    rest = HEAD_DIM - ROT_DIM
    cos_t = jnp.concatenate([cos, cos, jnp.ones((N_TOK, rest), F32)], axis=1)
    sin1_t = jnp.concatenate([-sin, jnp.zeros((N_TOK, HEAD_DIM - ROT_HALF), F32)], axis=1)
    sin2_t = jnp.concatenate([jnp.zeros((N_TOK, ROT_HALF), F32), sin,
                              jnp.zeros((N_TOK, rest), F32)], axis=1)
    return cos_t, sin1_t, sin2_t


def _experts(layer, h2, idx, rank, cnt, moe_w_gu, moe_b_gu, moe_w_down, moe_b_down):
    pos, plan = _plan(idx, rank, cnt)
    depth = moe_w_gu.shape[0]
    y_sorted = _moe(layer, h2, plan, moe_w_gu,
                    moe_b_gu.reshape(depth, N_EXPERTS, 1, 2 * D_FF), moe_w_down,
                    moe_b_down.reshape(depth, N_EXPERTS, 1, D_MODEL))
    return pos, y_sorted


def kernel(x, c, positions, norm_g, ada_w, ada_b, conv_w_in, conv_w, conv_w_out, kv_norm_g, w_kv,
           attn_w_q, attn_w_o, router_w, router_b, moe_w_gu, moe_b_gu, moe_w_down, moe_b_down,
           final_g):
    xf = x.reshape(N_TOK, D_MODEL)
    ada = _ada(c, ada_w, ada_b)
    mod = ada[:, :BATCH].reshape(2, BATCH, N_ADA, D_MODEL)
    cos_t, sin1_t, sin2_t = _rope_tables(positions)

    x1, h2, idx, wgt, rank, cnt = _mixer0(
        xf, mod[0], norm_g[0], conv_w_in[0].astype(BF16), conv_w[0], conv_w_out[0].astype(BF16),
        _split_hi_lo(router_w[0]), router_b[0].reshape(1, N_EXPERTS))
    pos, y_sorted = _experts(0, h2, idx, rank, cnt, moe_w_gu, moe_b_gu, moe_w_down, moe_b_down)

    w_qkv = jnp.concatenate([attn_w_q[0], w_kv], axis=1).astype(BF16)
    x2, qkv = _combine_qkv(pos, x1, mod[0], wgt, y_sorted, mod[1], norm_g[1], kv_norm_g, w_qkv,
                           cos_t, sin1_t, sin2_t)
    attn = _attention(qkv)
    x3, h2b, idx, wgt, rank, cnt = _oproj(
        attn, x2, mod[1], norm_g[1], attn_w_o[0].astype(BF16), _split_hi_lo(router_w[1]),
        router_b[1].reshape(1, N_EXPERTS))
    pos, y_sorted = _experts(1, h2b, idx, rank, cnt, moe_w_gu, moe_b_gu, moe_w_down, moe_b_down)
    out = _combine_final(pos, x3, mod[1], wgt, y_sorted, final_g)
    return out.reshape(BATCH, SEQ, D_MODEL)
```
